```python
import jax, jax.numpy as jnp
from jax import lax
import numpy as np

D_MODEL = 1024
BATCH = 8
SEQ = 8192
DEPTH = 1

N_HEADS = 8
QK_NOPE_DIM = 64
QK_ROPE_DIM = 32
V_HEAD_DIM = 64
Q_LORA_RANK = 256
KV_LORA_RANK = 128
ATTN_WIDTH = N_HEADS * V_HEAD_DIM
CONV_WIDTH = 512
CONV_K = 3
N_BRANCH = 2
BRANCH_WIDTH = 512
D_FF = 2816
N_SUB = 3
ROPE_THETA = 10000.0
Q_BLOCK = 128
EPS = 1e-6
IN_COLS = Q_LORA_RANK + KV_LORA_RANK + QK_ROPE_DIM + 3 * CONV_WIDTH + N_BRANCH * D_MODEL

kernel_name = "hybrid_mla_shortconv_macaron_adaln"


def rms_norm(x, w):
    xf = x.astype(jnp.float32)
    y = xf * lax.rsqrt(jnp.mean(xf * xf, axis=-1, keepdims=True) + EPS)
    return (y * w.astype(jnp.float32)).astype(x.dtype)


def modulate(x, w, shift, scale):
    return rms_norm(x, w) * (1 + scale[:, None, :]) + shift[:, None, :]


def swiglu(h, w13, w2):
    gu = jnp.einsum('bsd,df->bsf', h, w13)
    g, u = jnp.split(gu, 2, axis=-1)
    return jnp.einsum('bsf,fd->bsd', jax.nn.silu(g) * u, w2)


def rope_tables(positions):
    inv_freq = 1.0 / (ROPE_THETA ** (jnp.arange(0, QK_ROPE_DIM, 2, dtype=jnp.float32) / QK_ROPE_DIM))
    ang = positions.astype(jnp.float32)[..., None] * inv_freq
    return jnp.cos(ang), jnp.sin(ang)


def apply_rope(x, cos, sin):
    xf = x.astype(jnp.float32)
    x1, x2 = jnp.split(xf, 2, axis=-1)
    return jnp.concatenate([x1 * cos - x2 * sin, x2 * cos + x1 * sin], axis=-1).astype(x.dtype)


def causal_mla_attention(q_nope, q_rope, k_nope, k_rope, v):
    b, s, h, _ = q_nope.shape
    nb = s // Q_BLOCK
    scale = (QK_NOPE_DIM + QK_ROPE_DIM) ** -0.5
    key_idx = jnp.arange(s)

    def to_blocks(t):
        return jnp.moveaxis(t.reshape(b, nb, Q_BLOCK, *t.shape[2:]), 1, 0)

    def block(args):
        qn, qr, i = args
        sc = (jnp.einsum('bqhd,bkhd->bhqk', qn, k_nope, preferred_element_type=jnp.float32)
              + jnp.einsum('bqhr,bkr->bhqk', qr, k_rope, preferred_element_type=jnp.float32))
        q_idx = i * Q_BLOCK + jnp.arange(Q_BLOCK)
        mask = key_idx[None, :] <= q_idx[:, None]
        sc = jnp.where(mask, sc * scale, -jnp.inf)
        p = jax.nn.softmax(sc, axis=-1).astype(v.dtype)
        return jnp.einsum('bhqk,bkhd->bqhd', p, v)

    out = lax.map(block, (to_blocks(q_nope), to_blocks(q_rope), jnp.arange(nb)))
    return jnp.moveaxis(out, 0, 1).reshape(b, s, h * V_HEAD_DIM)


def hybrid_mixer(h, cos, sin, w_in, q_a_norm, w_uq, kv_a_norm, w_ukv,
                 q_norm_nope, k_norm_nope, q_norm_rope, k_norm_rope,
                 conv_w, w_branch, w_out):
    b, s, _ = h.shape
    proj = jnp.einsum('bsd,dn->bsn', h, w_in)
    i1 = Q_LORA_RANK
    i2 = i1 + KV_LORA_RANK
    i3 = i2 + QK_ROPE_DIM
    i4 = i3 + 3 * CONV_WIDTH
    c_q, c_kv, k_rope, conv_in, gate_logits = jnp.split(proj, [i1, i2, i3, i4], axis=-1)

    q = jnp.einsum('bsr,rn->bsn', rms_norm(c_q, q_a_norm), w_uq)
    q = q.reshape(b, s, N_HEADS, QK_NOPE_DIM + QK_ROPE_DIM)
    q_nope, q_rope = q[..., :QK_NOPE_DIM], q[..., QK_NOPE_DIM:]
    kv = jnp.einsum('bsr,rn->bsn', rms_norm(c_kv, kv_a_norm), w_ukv)
    kv = kv.reshape(b, s, N_HEADS, QK_NOPE_DIM + V_HEAD_DIM)
    k_nope, v = kv[..., :QK_NOPE_DIM], kv[..., QK_NOPE_DIM:]
    q_nope = rms_norm(q_nope, q_norm_nope)
    k_nope = rms_norm(k_nope, k_norm_nope)
    q_rope = apply_rope(rms_norm(q_rope, q_norm_rope), cos[:, :, None, :], sin[:, :, None, :])
    k_rope = apply_rope(rms_norm(k_rope, k_norm_rope), cos, sin)
    o_attn = causal_mla_attention(q_nope, q_rope, k_nope, k_rope, v)

    xv, gate_b, gate_c = jnp.split(conv_in, 3, axis=-1)
    u = gate_c * xv
    u = lax.conv_general_dilated(u, conv_w[:, None, :], window_strides=(1,),
                                 padding=[(CONV_K - 1, 0)],
                                 dimension_numbers=('NWC', 'WIO', 'NWC'),
                                 feature_group_count=CONV_WIDTH)
    o_conv = gate_b * u

    branches = jnp.stack([o_attn, o_conv], axis=2)
    y_br = jnp.einsum('bsgk,gkd->bsgd', branches, w_branch)
    gates = jax.nn.sigmoid(gate_logits.reshape(b, s, N_BRANCH, D_MODEL))
    merged = jnp.sum(gates * y_br, axis=2)
    return jnp.einsum('bsd,de->bse', merged, w_out)


def setup_inputs(seed: int = 0) -> dict:
    key = jax.random.key(seed)
    ks = jax.random.split(key, 20)
    f32 = jnp.float32

    def nrm(k, shape, fan_in):
        return jax.random.normal(k, shape, f32) * (fan_in ** -0.5)

    def gain(k, shape):
        return 1.0 + 0.02 * jax.random.normal(k, shape, f32)

    x = jax.random.normal(ks[0], (BATCH, SEQ, D_MODEL), f32)
    c = jax.random.normal(ks[1], (BATCH, D_MODEL), f32)
    offsets = jax.random.randint(ks[2], (BATCH, 1), 0, 1024, dtype=jnp.int32)
    positions = jnp.arange(SEQ, dtype=jnp.int32)[None, :] + offsets
    return {
        "x": x,
        "c": c,
        "positions": positions,
        "w_ada": nrm(ks[3], (DEPTH, D_MODEL, 3 * N_SUB * D_MODEL), D_MODEL),
        "b_ada": 0.01 * jax.random.normal(ks[4], (DEPTH, 3 * N_SUB * D_MODEL), f32),
        "norm_w": gain(ks[5], (DEPTH, N_SUB, D_MODEL)),
        "ffn_w13": nrm(ks[6], (DEPTH, 2, D_MODEL, 2 * D_FF), D_MODEL),
        "ffn_w2": nrm(ks[7], (DEPTH, 2, D_FF, D_MODEL), D_FF),
        "w_in": nrm(ks[8], (DEPTH, D_MODEL, IN_COLS), D_MODEL),
        "q_a_norm": gain(ks[9], (DEPTH, Q_LORA_RANK)),
        "w_uq": nrm(ks[10], (DEPTH, Q_LORA_RANK, N_HEADS * (QK_NOPE_DIM + QK_ROPE_DIM)), Q_LORA_RANK),
        "kv_a_norm": gain(ks[11], (DEPTH, KV_LORA_RANK)),
        "w_ukv": nrm(ks[12], (DEPTH, KV_LORA_RANK, N_HEADS * (QK_NOPE_DIM + V_HEAD_DIM)), KV_LORA_RANK),
        "q_norm_nope": gain(ks[13], (DEPTH, QK_NOPE_DIM)),
        "k_norm_nope": gain(ks[14], (DEPTH, QK_NOPE_DIM)),
        "q_norm_rope": gain(ks[15], (DEPTH, QK_ROPE_DIM)),
        "k_norm_rope": gain(ks[16], (DEPTH, QK_ROPE_DIM)),
        "conv_w": nrm(ks[17], (DEPTH, CONV_K, CONV_WIDTH), CONV_K),
        "w_branch": nrm(ks[18], (DEPTH, N_BRANCH, BRANCH_WIDTH, D_MODEL), BRANCH_WIDTH),
        "w_out": nrm(ks[19], (DEPTH, D_MODEL, D_MODEL), D_MODEL),
    }


def reference(x, c, positions, w_ada, b_ada, norm_w, ffn_w13, ffn_w2, w_in,
              q_a_norm, w_uq, kv_a_norm, w_ukv, q_norm_nope, k_norm_nope,
              q_norm_rope, k_norm_rope, conv_w, w_branch, w_out):
    b = x.shape[0]
    cos, sin = rope_tables(positions)
    c_act = jax.nn.silu(c)
    for l in range(DEPTH):
        mod = (jnp.einsum('bd,dn->bn', c_act, w_ada[l]) + b_ada[l]).reshape(b, N_SUB, 3, D_MODEL)
        h = modulate(x, norm_w[l, 0], mod[:, 0, 0], mod[:, 0, 1])
        x = x + 0.5 * mod[:, 0, 2][:, None, :] * swiglu(h, ffn_w13[l, 0], ffn_w2[l, 0])
        h = modulate(x, norm_w[l, 1], mod[:, 1, 0], mod[:, 1, 1])
        y = hybrid_mixer(h, cos, sin, w_in[l], q_a_norm[l], w_uq[l], kv_a_norm[l], w_ukv[l],
                         q_norm_nope[l], k_norm_nope[l], q_norm_rope[l], k_norm_rope[l],
                         conv_w[l], w_branch[l], w_out[l])
        x = x + mod[:, 1, 2][:, None, :] * y
        h = modulate(x, norm_w[l, 2], mod[:, 2, 0], mod[:, 2, 1])
        x = x + 0.5 * mod[:, 2, 2][:, None, :] * swiglu(h, ffn_w13[l, 1], ffn_w2[l, 1])
    return x
```

```python
import functools
import math

import jax
import jax.numpy as jnp
from jax import lax
from jax.experimental import pallas as pl
from jax.experimental.pallas import tpu as pltpu

D_MODEL = 1024
N_HEADS = 8
QK_NOPE_DIM = 64
QK_ROPE_DIM = 32
V_HEAD_DIM = 64
Q_LORA_RANK = 256
KV_LORA_RANK = 128
CONV_WIDTH = 512
CONV_K = 3
D_FF = 2816
N_SUB = 3
ROPE_THETA = 10000.0
EPS = 1e-6

LANES = 128
HEAD_PAD = 128
ROPE_HALF = QK_ROPE_DIM // 2
V_ROWS = V_HEAD_DIM + 16
A_ROWS = Q_LORA_RANK + KV_LORA_RANK + QK_ROPE_DIM
CG_COLS = 3 * CONV_WIDTH + 2 * D_MODEL
VMEM_LIMIT = 56 * 1024 * 1024

FFN_TM = 256
MIX_TM = 256
POST_TM = 512
ATT_TQ = 256
ATT_HG = 4

Q_SCALE = (QK_NOPE_DIM + QK_ROPE_DIM) ** -0.5 * math.log2(math.e)
NEG_BIG = -1e30


def _resident(shape):
    n = len(shape)
    return pl.BlockSpec(shape, lambda *_: (0,) * n, pipeline_mode=pl.Buffered(1))


def _modulated_norm(x, nw, shift, scale):
    ms = jnp.mean(x * x, axis=-1, keepdims=True)
    return x * lax.rsqrt(ms + EPS) * (nw * (1.0 + scale)) + shift


def _rope_kernel(pos_ref, invf_ref, cos_ref, sin_ref):
    ang = pos_ref[0].astype(jnp.float32) * invf_ref[...]
    cos_ref[0] = jnp.cos(ang)
    sin_ref[0] = jnp.sin(ang)


def _rope_tables(positions):
    b, s = positions.shape
    inv_freq = 1.0 / (ROPE_THETA ** (jnp.arange(0, QK_ROPE_DIM, 2, dtype=jnp.float32) / QK_ROPE_DIM))
    invf = jnp.broadcast_to(inv_freq[:, None], (ROPE_HALF, s))
    out = jax.ShapeDtypeStruct((b, ROPE_HALF, s), jnp.float32)
    return pl.pallas_call(
        _rope_kernel,
        grid=(b,),
        in_specs=[pl.BlockSpec((1, 1, s), lambda i: (i, 0, 0)),
                  pl.BlockSpec((ROPE_HALF, s), lambda i: (0, 0))],
        out_specs=[pl.BlockSpec((1, ROPE_HALF, s), lambda i: (i, 0, 0))] * 2,
        out_shape=[out, out],
        name="rope_tables",
    )(positions.reshape(b, 1, s), invf)


def _mod_kernel(c_ref, w_ref, b_ref, o_ref):
    c = c_ref[...]
    c_act = c * jax.nn.sigmoid(c)
    o_ref[...] = jnp.dot(c_act.astype(jnp.bfloat16), w_ref[...].astype(jnp.bfloat16),
                         preferred_element_type=jnp.float32) + b_ref[...]


def _modulation(c, w_ada, b_ada):
    b, d = c.shape
    n = w_ada.shape[1]
    tn = 1024
    return pl.pallas_call(
        _mod_kernel,
        grid=(n // tn,),
        in_specs=[pl.BlockSpec((b, d), lambda j: (0, 0)),
                  pl.BlockSpec((d, tn), lambda j: (0, j)),
                  pl.BlockSpec((1, tn), lambda j: (0, j))],
        out_specs=pl.BlockSpec((b, tn), lambda j: (0, j)),
        out_shape=jax.ShapeDtypeStruct((b, n), jnp.float32),
        name="adaln_mod",
    )(c, w_ada, b_ada.reshape(1, n))


def _ffn_kernel(sub, x_ref, mod_ref, nw_ref, w13_ref, w2_ref, o_ref):
    x = x_ref[0]
    shift = mod_ref[0, 3 * sub + 0:3 * sub + 1, :]
    scale = mod_ref[0, 3 * sub + 1:3 * sub + 2, :]
    gate = mod_ref[0, 3 * sub + 2:3 * sub + 3, :]
    h = _modulated_norm(x, nw_ref[sub:sub + 1, :], shift, scale).astype(jnp.bfloat16)
    gu = jnp.dot(h, w13_ref[...], preferred_element_type=jnp.float32)
    g = gu[:, :D_FF]
    u = gu[:, D_FF:]
    a = (g * jax.nn.sigmoid(g) * u).astype(jnp.bfloat16)
    y = jnp.dot(a, w2_ref[...], preferred_element_type=jnp.float32)
    o_ref[0] = x + (0.5 * gate) * y


def _ffn(x, mod, norm_w, w13, w2, sub):
    b, s, d = x.shape
    tm = FFN_TM
    return pl.pallas_call(
        functools.partial(_ffn_kernel, sub),
        grid=(b, s // tm),
        in_specs=[pl.BlockSpec((1, tm, d), lambda i, j: (i, j, 0)),
                  pl.BlockSpec((1, 3 * N_SUB, d), lambda i, j: (i, 0, 0)),
                  _resident(norm_w.shape),
                  _resident(w13.shape),
                  _resident(w2.shape)],
        out_specs=pl.BlockSpec((1, tm, d), lambda i, j: (i, j, 0)),
        out_shape=jax.ShapeDtypeStruct(x.shape, jnp.float32),
        compiler_params=pltpu.CompilerParams(
            dimension_semantics=("arbitrary", "arbitrary"), vmem_limit_bytes=VMEM_LIMIT),
        name=f"ffn{sub}",
    )(x, mod, norm_w, w13, w2)


def _rms_rows(x, w):
    ms = jnp.mean(x * x, axis=0, keepdims=True)
    return x * lax.rsqrt(ms + EPS) * w


def _rope_rows(y, cos, sin):
    y1 = y[:ROPE_HALF]
    y2 = y[ROPE_HALF:]
    return y1 * cos - y2 * sin, y2 * cos + y1 * sin


def _mix_kernel(x_ref, mod_ref, nw_ref, cos_ref, sin_ref, wa_ref, wcg_ref, wuq_ref, wukv_ref,
                colw_ref, convw_ref, wb1_ref,
                q_ref, k_ref, v_ref, g0_ref, mc_ref, tail_ref):
    tm = x_ref.shape[1]
    rep = tm // LANES
    j = pl.program_id(1)

    x = x_ref[0]
    h = _modulated_norm(x, nw_ref[1:2, :], mod_ref[0, 3:4, :], mod_ref[0, 4:5, :]).astype(jnp.bfloat16)

    cg = jnp.dot(h, wcg_ref[...], preferred_element_type=jnp.float32)
    xv = cg[:, 0:CONV_WIDTH]
    gate_b = cg[:, CONV_WIDTH:2 * CONV_WIDTH]
    gate_c = cg[:, 2 * CONV_WIDTH:3 * CONV_WIDTH]
    u = gate_c * xv

    @pl.when(j == 0)
    def _():
        tail_ref[...] = jnp.zeros_like(tail_ref)

    p1 = tail_ref[7:8, :]
    p2 = tail_ref[6:7, :]
    row = lax.broadcasted_iota(jnp.int32, u.shape, 0)
    r1 = jnp.where(row == 0, p1, pltpu.roll(u, 1, 0))
    r2 = jnp.where(row == 0, p2, jnp.where(row == 1, p1, pltpu.roll(u, 2, 0)))
    tail_ref[...] = u[tm - 8:tm, :]
    conv = convw_ref[0:1, :] * r2 + convw_ref[1:2, :] * r1 + convw_ref[2:3, :] * u
    o_conv = (gate_b * conv).astype(jnp.bfloat16)
    y_conv = jnp.dot(o_conv, wb1_ref[...], preferred_element_type=jnp.float32)
    off = 3 * CONV_WIDTH
    g0 = jax.nn.sigmoid(cg[:, off:off + D_MODEL])
    g1 = jax.nn.sigmoid(cg[:, off + D_MODEL:off + 2 * D_MODEL])
    g0_ref[0] = g0.astype(jnp.bfloat16)
    mc_ref[0] = (g1 * y_conv).astype(jnp.bfloat16)

    zt = lax.dot_general(wa_ref[...], h, (((1,), (1,)), ((), ())),
                         preferred_element_type=jnp.float32)

    def colw(lo, n):
        return jnp.tile(colw_ref[lo:lo + n, :], (1, rep))

    o_qa, o_kva = 0, Q_LORA_RANK
    o_qn = o_kva + KV_LORA_RANK
    o_kn = o_qn + QK_NOPE_DIM
    o_qr = o_kn + QK_NOPE_DIM
    o_kr = o_qr + QK_ROPE_DIM

    cq = _rms_rows(zt[0:Q_LORA_RANK], colw(o_qa, Q_LORA_RANK)).astype(jnp.bfloat16)
    ckv = _rms_rows(zt[Q_LORA_RANK:Q_LORA_RANK + KV_LORA_RANK], colw(o_kva, KV_LORA_RANK)).astype(jnp.bfloat16)
    kr_raw = zt[Q_LORA_RANK + KV_LORA_RANK:A_ROWS]
    cos = cos_ref[0]
    sin = sin_ref[0]
    kr1, kr2 = _rope_rows(_rms_rows(kr_raw, colw(o_kr, QK_ROPE_DIM)), cos, sin)

    qt = jnp.dot(wuq_ref[...], cq, preferred_element_type=jnp.float32)
    kvt = jnp.dot(wukv_ref[...], ckv, preferred_element_type=jnp.float32)
    wqn = colw(o_qn, QK_NOPE_DIM)
    wkn = colw(o_kn, QK_NOPE_DIM)
    wqr = colw(o_qr, QK_ROPE_DIM)
    zpad = jnp.zeros((HEAD_PAD - QK_NOPE_DIM - QK_ROPE_DIM, tm), jnp.float32)
    ones = jnp.ones((V_ROWS - V_HEAD_DIM, tm), jnp.bfloat16)
    for hd in range(N_HEADS):
        base = hd * HEAD_PAD
        qn = _rms_rows(qt[base:base + QK_NOPE_DIM], wqn)
        qr1, qr2 = _rope_rows(
            _rms_rows(qt[base + QK_NOPE_DIM:base + QK_NOPE_DIM + QK_ROPE_DIM], wqr), cos, sin)
        q_ref[0, hd, 0:QK_NOPE_DIM, :] = (qn * Q_SCALE).astype(jnp.bfloat16)
        q_ref[0, hd, QK_NOPE_DIM:QK_NOPE_DIM + ROPE_HALF, :] = (qr1 * Q_SCALE).astype(jnp.bfloat16)
        q_ref[0, hd, QK_NOPE_DIM + ROPE_HALF:QK_NOPE_DIM + QK_ROPE_DIM, :] = (qr2 * Q_SCALE).astype(jnp.bfloat16)
        q_ref[0, hd, QK_NOPE_DIM + QK_ROPE_DIM:HEAD_PAD, :] = zpad.astype(jnp.bfloat16)

        kn = _rms_rows(kvt[base:base + QK_NOPE_DIM], wkn)
        kt = jnp.concatenate([kn, kr1, kr2, zpad], axis=0)
        k_ref[0, hd] = kt.T.astype(jnp.bfloat16)

        v_ref[0, hd, 0:V_HEAD_DIM, :] = kvt[base + QK_NOPE_DIM:base + HEAD_PAD].astype(jnp.bfloat16)
        v_ref[0, hd, V_HEAD_DIM:V_ROWS, :] = ones


def _mixer_inputs(x, mod, norm_w, cos_t, sin_t, wa_t, wcg, wuq_t, wukv_t, colw, conv_w, wb1):
    b, s, d = x.shape
    tm = MIX_TM
    bf = jnp.bfloat16
    out_shape = [
        jax.ShapeDtypeStruct((b, N_HEADS, HEAD_PAD, s), bf),
        jax.ShapeDtypeStruct((b, N_HEADS, s, HEAD_PAD), bf),
        jax.ShapeDtypeStruct((b, N_HEADS, V_ROWS, s), bf),
        jax.ShapeDtypeStruct((b, s, d), bf),
        jax.ShapeDtypeStruct((b, s, d), bf),
    ]
    out_specs = [
        pl.BlockSpec((1, N_HEADS, HEAD_PAD, tm), lambda i, j: (i, 0, 0, j)),
        pl.BlockSpec((1, N_HEADS, tm, HEAD_PAD), lambda i, j: (i, 0, j, 0)),
        pl.BlockSpec((1, N_HEADS, V_ROWS, tm), lambda i, j: (i, 0, 0, j)),
        pl.BlockSpec((1, tm, d), lambda i, j: (i, j, 0)),
        pl.BlockSpec((1, tm, d), lambda i, j: (i, j, 0)),
    ]
    in_specs = [
        pl.BlockSpec((1, tm, d), lambda i, j: (i, j, 0)),
        pl.BlockSpec((1, 3 * N_SUB, d), lambda i, j: (i, 0, 0)),
        _resident(norm_w.shape),
        pl.BlockSpec((1, ROPE_HALF, tm), lambda i, j: (i, 0, j)),
        pl.BlockSpec((1, ROPE_HALF, tm), lambda i, j: (i, 0, j)),
        _resident(wa_t.shape), _resident(wcg.shape), _resident(wuq_t.shape), _resident(wukv_t.shape),
        _resident(colw.shape), _resident(conv_w.shape), _resident(wb1.shape),
    ]
    return pl.pallas_call(
        _mix_kernel,
        grid=(b, s // tm),
        in_specs=in_specs,
        out_specs=out_specs,
        out_shape=out_shape,
        scratch_shapes=[pltpu.VMEM((8, CONV_WIDTH), jnp.float32)],
        compiler_params=pltpu.CompilerParams(
            dimension_semantics=("arbitrary", "arbitrary"), vmem_limit_bytes=VMEM_LIMIT),
        name="mixer_inputs",
    )(x, mod, norm_w, cos_t, sin_t, wa_t, wcg, wuq_t, wukv_t, colw, conv_w, wb1)


def _attn_kernel(q_ref, k_ref, v_ref, o_ref, m_ref, acc_ref):
    tq = q_ref.shape[3]
    i = pl.program_id(2)
    diag = pl.multiple_of(i * tq, tq)

    key = lax.broadcasted_iota(jnp.int32, (tq, tq), 0)
    qry = lax.broadcasted_iota(jnp.int32, (tq, tq), 1)
    visible = key <= qry
    for hd in range(ATT_HG):
        s = jnp.dot(k_ref[0, hd, pl.ds(diag, tq), :], q_ref[0, hd],
                    preferred_element_type=jnp.float32)
        s = jnp.where(visible, s, NEG_BIG)
        m = jnp.max(s, axis=0, keepdims=True)
        p = jnp.exp2(s - m).astype(jnp.bfloat16)
        acc_ref[hd] = jnp.dot(v_ref[0, hd, :, pl.ds(diag, tq)], p, preferred_element_type=jnp.float32)
        m_ref[hd] = m

    def body(j, carry):
        start = pl.multiple_of(j * tq, tq)
        for hd in range(ATT_HG):
            s = jnp.dot(k_ref[0, hd, pl.ds(start, tq), :], q_ref[0, hd],
                        preferred_element_type=jnp.float32)
            m_old = m_ref[hd]
            m_new = jnp.maximum(m_old, jnp.max(s, axis=0, keepdims=True))
            p = jnp.exp2(s - m_new).astype(jnp.bfloat16)
            alpha = jnp.exp2(m_old - m_new)
            pv = jnp.dot(v_ref[0, hd, :, pl.ds(start, tq)], p, preferred_element_type=jnp.float32)
            acc_ref[hd] = alpha * acc_ref[hd] + pv
            m_ref[hd] = m_new
        return carry

    lax.fori_loop(0, i, body, 0)

    for hd in range(ATT_HG):
        acc = acc_ref[hd]
        o_ref[0, hd] = (acc[0:V_HEAD_DIM] / acc[V_HEAD_DIM:V_HEAD_DIM + 1]).astype(o_ref.dtype)


def _attention(q_t, k, v_t):
    b, nh, _, s = q_t.shape
    tq = ATT_TQ
    hg = ATT_HG
    return pl.pallas_call(
        _attn_kernel,
        grid=(b, nh // hg, s // tq),
        in_specs=[pl.BlockSpec((1, hg, HEAD_PAD, tq), lambda bi, g, i: (bi, g, 0, i)),
                  pl.BlockSpec((1, hg, s, HEAD_PAD), lambda bi, g, i: (bi, g, 0, 0)),
                  pl.BlockSpec((1, hg, V_ROWS, s), lambda bi, g, i: (bi, g, 0, 0))],
        out_specs=pl.BlockSpec((1, hg, V_HEAD_DIM, tq), lambda bi, g, i: (bi, g, 0, i)),
        out_shape=jax.ShapeDtypeStruct((b, nh, V_HEAD_DIM, s), jnp.bfloat16),
        scratch_shapes=[pltpu.VMEM((hg, 1, tq), jnp.float32),
                        pltpu.VMEM((hg, V_ROWS, tq), jnp.float32)],
        compiler_params=pltpu.CompilerParams(
            dimension_semantics=("arbitrary", "arbitrary", "arbitrary"), vmem_limit_bytes=VMEM_LIMIT),
        name="mla_attention",
    )(q_t, k, v_t)


def _post_kernel(x_ref, mod_ref, o_ref, g0_ref, mc_ref, wb0_ref, wout_ref, y_ref):
    o_t = o_ref[0]
    y_attn = lax.dot_general(o_t, wb0_ref[...], (((0,), (0,)), ((), ())),
                             preferred_element_type=jnp.float32)
    merged = g0_ref[0].astype(jnp.float32) * y_attn + mc_ref[0].astype(jnp.float32)
    y = jnp.dot(merged.astype(jnp.bfloat16), wout_ref[...], preferred_element_type=jnp.float32)
    y_ref[0] = x_ref[0] + mod_ref[0, 5:6, :] * y


def _post(x, mod, o_t, g0, mc, wb0, wout):
    b, s, d = x.shape
    tm = POST_TM
    rows = o_t.shape[1]
    return pl.pallas_call(
        _post_kernel,
        grid=(b, s // tm),
        in_specs=[pl.BlockSpec((1, tm, d), lambda i, j: (i, j, 0)),
                  pl.BlockSpec((1, 3 * N_SUB, d), lambda i, j: (i, 0, 0)),
                  pl.BlockSpec((1, rows, tm), lambda i, j: (i, 0, j)),
                  pl.BlockSpec((1, tm, d), lambda i, j: (i, j, 0)),
                  pl.BlockSpec((1, tm, d), lambda i, j: (i, j, 0)),
                  _resident(wb0.shape), _resident(wout.shape)],
        out_specs=pl.BlockSpec((1, tm, d), lambda i, j: (i, j, 0)),
        out_shape=jax.ShapeDtypeStruct(x.shape, jnp.float32),
        compiler_params=pltpu.CompilerParams(
            dimension_semantics=("arbitrary", "arbitrary"), vmem_limit_bytes=VMEM_LIMIT),
        name="merge_out",
    )(x, mod, o_t, g0, mc, wb0, wout)


def _bcast_cols(v):
    return jnp.broadcast_to(v[:, None], (v.shape[0], LANES))


def kernel(x, c, positions, w_ada, b_ada, norm_w, ffn_w13, ffn_w2, w_in, q_a_norm, w_uq, kv_a_norm,
           w_ukv, q_norm_nope, k_norm_nope, q_norm_rope, k_norm_rope, conv_w, w_branch, w_out):
    bf = jnp.bfloat16
    b, s, d = x.shape
    depth = w_ada.shape[0]
    cos_t, sin_t = _rope_tables(positions)
    for l in range(depth):
        mod = _modulation(c, w_ada[l], b_ada[l]).reshape(b, 3 * N_SUB, d)

        wi = w_in[l]
        wa_t = wi[:, :A_ROWS].T.astype(bf)
        wcg = wi[:, A_ROWS:].astype(bf)
        wuq = w_uq[l].reshape(Q_LORA_RANK, N_HEADS, QK_NOPE_DIM + QK_ROPE_DIM)
        wuq = jnp.pad(wuq, ((0, 0), (0, 0), (0, HEAD_PAD - QK_NOPE_DIM - QK_ROPE_DIM)))
        wuq_t = wuq.reshape(Q_LORA_RANK, N_HEADS * HEAD_PAD).T.astype(bf)
        wukv_t = w_ukv[l].T.astype(bf)
        colw = jnp.concatenate([_bcast_cols(q_a_norm[l]), _bcast_cols(kv_a_norm[l]),
                                _bcast_cols(q_norm_nope[l]), _bcast_cols(k_norm_nope[l]),
                                _bcast_cols(q_norm_rope[l]), _bcast_cols(k_norm_rope[l])], axis=0)

        x = _ffn(x, mod, norm_w[l], ffn_w13[l, 0].astype(bf), ffn_w2[l, 0].astype(bf), 0)
        q_t, k, v_t, g0, mc = _mixer_inputs(x, mod, norm_w[l], cos_t, sin_t, wa_t, wcg, wuq_t, wukv_t,
                                            colw, conv_w[l], w_branch[l, 1].astype(bf))
        o_t = _attention(q_t, k, v_t)
        x = _post(x, mod, o_t.reshape(b, N_HEADS * V_HEAD_DIM, s), g0, mc,
                  w_branch[l, 0].astype(bf), w_out[l].astype(bf))
        x = _ffn(x, mod, norm_w[l], ffn_w13[l, 1].astype(bf), ffn_w2[l, 1].astype(bf), 2)
    return x
```

```python
import functools
import math

import jax
import jax.numpy as jnp
from jax import lax
from jax.experimental import pallas as pl
from jax.experimental.pallas import tpu as pltpu

D_MODEL = 1024
N_HEADS = 8
QK_NOPE_DIM = 64
QK_ROPE_DIM = 32
V_HEAD_DIM = 64
Q_LORA_RANK = 256
KV_LORA_RANK = 128
CONV_WIDTH = 512
CONV_K = 3
D_FF = 2816
N_SUB = 3
ROPE_THETA = 10000.0
EPS = 1e-6

LANES = 128
HEAD_PAD = 128
ROPE_HALF = QK_ROPE_DIM // 2
V_ROWS = V_HEAD_DIM + 16
A_ROWS = Q_LORA_RANK + KV_LORA_RANK + QK_ROPE_DIM
CG_COLS = 3 * CONV_WIDTH + 2 * D_MODEL
VMEM_LIMIT = 56 * 1024 * 1024

FFN_TM = 512
MIX_TM = 512
POST_TM = 512
ATT_TQ = 512
ATT_TK = 512
ATT_HG = 4

Q_SCALE = (QK_NOPE_DIM + QK_ROPE_DIM) ** -0.5 * math.log2(math.e)
NEG_BIG = -1e30


def _resident(shape):
    n = len(shape)
    return pl.BlockSpec(shape, lambda *_: (0,) * n, pipeline_mode=pl.Buffered(1))


def _modulated_norm(x, nw, shift, scale):
    ms = jnp.mean(x * x, axis=-1, keepdims=True)
    return x * lax.rsqrt(ms + EPS) * (nw * (1.0 + scale)) + shift


def _rope_kernel(pos_ref, invf_ref, cos_ref, sin_ref):
    ang = pos_ref[0].astype(jnp.float32) * invf_ref[...]
    cos_ref[0] = jnp.cos(ang)
    sin_ref[0] = jnp.sin(ang)


def _rope_tables(positions):
    b, s = positions.shape
    inv_freq = 1.0 / (ROPE_THETA ** (jnp.arange(0, QK_ROPE_DIM, 2, dtype=jnp.float32) / QK_ROPE_DIM))
    invf = jnp.broadcast_to(inv_freq[:, None], (ROPE_HALF, s))
    out = jax.ShapeDtypeStruct((b, ROPE_HALF, s), jnp.float32)
    return pl.pallas_call(
        _rope_kernel,
        grid=(b,),
        in_specs=[pl.BlockSpec((1, 1, s), lambda i: (i, 0, 0)),
                  pl.BlockSpec((ROPE_HALF, s), lambda i: (0, 0))],
        out_specs=[pl.BlockSpec((1, ROPE_HALF, s), lambda i: (i, 0, 0))] * 2,
        out_shape=[out, out],
        name="rope_tables",
    )(positions.reshape(b, 1, s), invf)


def _mod_kernel(c_ref, w_ref, b_ref, o_ref):
    c = c_ref[...]
    c_act = c * jax.nn.sigmoid(c)
    o_ref[...] = jnp.dot(c_act.astype(jnp.bfloat16), w_ref[...].astype(jnp.bfloat16),
                         preferred_element_type=jnp.float32) + b_ref[...]


def _modulation(c, w_ada, b_ada):
    b, d = c.shape
    n = w_ada.shape[1]
    tn = 1024
    return pl.pallas_call(
        _mod_kernel,
        grid=(n // tn,),
        in_specs=[pl.BlockSpec((b, d), lambda j: (0, 0)),
                  pl.BlockSpec((d, tn), lambda j: (0, j)),
                  pl.BlockSpec((1, tn), lambda j: (0, j))],
        out_specs=pl.BlockSpec((b, tn), lambda j: (0, j)),
        out_shape=jax.ShapeDtypeStruct((b, n), jnp.float32),
        name="adaln_mod",
    )(c, w_ada, b_ada.reshape(1, n))


def _ffn_kernel(sub, x_ref, mod_ref, nw_ref, w13_ref, w2_ref, o_ref):
    x = x_ref[0]
    shift = mod_ref[0, 3 * sub + 0:3 * sub + 1, :]
    scale = mod_ref[0, 3 * sub + 1:3 * sub + 2, :]
    gate = mod_ref[0, 3 * sub + 2:3 * sub + 3, :]
    h = _modulated_norm(x, nw_ref[sub:sub + 1, :], shift, scale).astype(jnp.bfloat16)
    gu = jnp.dot(h, w13_ref[...], preferred_element_type=jnp.float32)
    g = gu[:, :D_FF]
    u = gu[:, D_FF:]
    a = (g * jax.nn.sigmoid(g) * u).astype(jnp.bfloat16)
    y = jnp.dot(a, w2_ref[...], preferred_element_type=jnp.float32)
    o_ref[0] = x + (0.5 * gate) * y


def _ffn(x, mod, norm_w, w13, w2, sub):
    b, s, d = x.shape
    tm = FFN_TM
    return pl.pallas_call(
        functools.partial(_ffn_kernel, sub),
        grid=(b, s // tm),
        in_specs=[pl.BlockSpec((1, tm, d), lambda i, j: (i, j, 0)),
                  pl.BlockSpec((1, 3 * N_SUB, d), lambda i, j: (i, 0, 0)),
                  _resident(norm_w.shape),
                  _resident(w13.shape),
                  _resident(w2.shape)],
        out_specs=pl.BlockSpec((1, tm, d), lambda i, j: (i, j, 0)),
        out_shape=jax.ShapeDtypeStruct(x.shape, jnp.float32),
        compiler_params=pltpu.CompilerParams(
            dimension_semantics=("arbitrary", "arbitrary"), vmem_limit_bytes=VMEM_LIMIT),
        name=f"ffn{sub}",
    )(x, mod, norm_w, w13, w2)


def _rms_rows(x, w):
    ms = jnp.mean(x * x, axis=0, keepdims=True)
    return x * lax.rsqrt(ms + EPS) * w


def _rope_rows(y, cos, sin):
    y1 = y[:ROPE_HALF]
    y2 = y[ROPE_HALF:]
    return y1 * cos - y2 * sin, y2 * cos + y1 * sin


def _mix_kernel(x_ref, mod_ref, nw_ref, cos_ref, sin_ref, wa_ref, wcg_ref, wuq_ref, wukv_ref,
                colw_ref, convw_ref, wb1_ref,
                q_ref, k_ref, v_ref, g0_ref, mc_ref, tail_ref):
    tm = x_ref.shape[1]
    rep = tm // LANES
    j = pl.program_id(1)

    x = x_ref[0]
    h = _modulated_norm(x, nw_ref[1:2, :], mod_ref[0, 3:4, :], mod_ref[0, 4:5, :]).astype(jnp.bfloat16)

    cg = jnp.dot(h, wcg_ref[...], preferred_element_type=jnp.float32)
    xv = cg[:, 0:CONV_WIDTH]
    gate_b = cg[:, CONV_WIDTH:2 * CONV_WIDTH]
    gate_c = cg[:, 2 * CONV_WIDTH:3 * CONV_WIDTH]
    u = gate_c * xv

    @pl.when(j == 0)
    def _():
        tail_ref[...] = jnp.zeros_like(tail_ref)

    p1 = tail_ref[7:8, :]
    p2 = tail_ref[6:7, :]
    row = lax.broadcasted_iota(jnp.int32, u.shape, 0)
    r1 = jnp.where(row == 0, p1, pltpu.roll(u, 1, 0))
    r2 = jnp.where(row == 0, p2, jnp.where(row == 1, p1, pltpu.roll(u, 2, 0)))
    tail_ref[...] = u[tm - 8:tm, :]
    conv = convw_ref[0:1, :] * r2 + convw_ref[1:2, :] * r1 + convw_ref[2:3, :] * u
    o_conv = (gate_b * conv).astype(jnp.bfloat16)
    y_conv = jnp.dot(o_conv, wb1_ref[...], preferred_element_type=jnp.float32)
    off = 3 * CONV_WIDTH
    g0 = jax.nn.sigmoid(cg[:, off:off + D_MODEL])
    g1 = jax.nn.sigmoid(cg[:, off + D_MODEL:off + 2 * D_MODEL])
    g0_ref[0] = g0.astype(jnp.bfloat16)
    mc_ref[0] = (g1 * y_conv).astype(jnp.bfloat16)

    zt = lax.dot_general(wa_ref[...], h, (((1,), (1,)), ((), ())),
                         preferred_element_type=jnp.float32)

    def colw(lo, n):
        return jnp.tile(colw_ref[lo:lo + n, :], (1, rep))

    o_qa, o_kva = 0, Q_LORA_RANK
    o_qn = o_kva + KV_LORA_RANK
    o_kn = o_qn + QK_NOPE_DIM
    o_qr = o_kn + QK_NOPE_DIM
    o_kr = o_qr + QK_ROPE_DIM

    cq = _rms_rows(zt[0:Q_LORA_RANK], colw(o_qa, Q_LORA_RANK)).astype(jnp.bfloat16)
    ckv = _rms_rows(zt[Q_LORA_RANK:Q_LORA_RANK + KV_LORA_RANK], colw(o_kva, KV_LORA_RANK)).astype(jnp.bfloat16)
    kr_raw = zt[Q_LORA_RANK + KV_LORA_RANK:A_ROWS]
    cos = cos_ref[0]
    sin = sin_ref[0]
    kr1, kr2 = _rope_rows(_rms_rows(kr_raw, colw(o_kr, QK_ROPE_DIM)), cos, sin)

    qt = jnp.dot(wuq_ref[...], cq, preferred_element_type=jnp.float32)
    kvt = jnp.dot(wukv_ref[...], ckv, preferred_element_type=jnp.float32)
    wqn = colw(o_qn, QK_NOPE_DIM)
    wkn = colw(o_kn, QK_NOPE_DIM)
    wqr = colw(o_qr, QK_ROPE_DIM)
    zpad = jnp.zeros((HEAD_PAD - QK_NOPE_DIM - QK_ROPE_DIM, tm), jnp.float32)
    ones = jnp.ones((V_ROWS - V_HEAD_DIM, tm), jnp.bfloat16)
    for hd in range(N_HEADS):
        base = hd * HEAD_PAD
        qn = _rms_rows(qt[base:base + QK_NOPE_DIM], wqn)
        qr1, qr2 = _rope_rows(
            _rms_rows(qt[base + QK_NOPE_DIM:base + QK_NOPE_DIM + QK_ROPE_DIM], wqr), cos, sin)
        q_ref[0, hd, 0:QK_NOPE_DIM, :] = (qn * Q_SCALE).astype(jnp.bfloat16)
        q_ref[0, hd, QK_NOPE_DIM:QK_NOPE_DIM + ROPE_HALF, :] = (qr1 * Q_SCALE).astype(jnp.bfloat16)
        q_ref[0, hd, QK_NOPE_DIM + ROPE_HALF:QK_NOPE_DIM + QK_ROPE_DIM, :] = (qr2 * Q_SCALE).astype(jnp.bfloat16)
        q_ref[0, hd, QK_NOPE_DIM + QK_ROPE_DIM:HEAD_PAD, :] = zpad.astype(jnp.bfloat16)

        kn = _rms_rows(kvt[base:base + QK_NOPE_DIM], wkn)
        kt = jnp.concatenate([kn, kr1, kr2, zpad], axis=0)
        k_ref[0, hd] = kt.T.astype(jnp.bfloat16)

        v_ref[0, hd, 0:V_HEAD_DIM, :] = kvt[base + QK_NOPE_DIM:base + HEAD_PAD].astype(jnp.bfloat16)
        v_ref[0, hd, V_HEAD_DIM:V_ROWS, :] = ones


def _mixer_inputs(x, mod, norm_w, cos_t, sin_t, wa_t, wcg, wuq_t, wukv_t, colw, conv_w, wb1):
    b, s, d = x.shape
    tm = MIX_TM
    bf = jnp.bfloat16
    out_shape = [
        jax.ShapeDtypeStruct((b, N_HEADS, HEAD_PAD, s), bf),
        jax.ShapeDtypeStruct((b, N_HEADS, s, HEAD_PAD), bf),
        jax.ShapeDtypeStruct((b, N_HEADS, V_ROWS, s), bf),
        jax.ShapeDtypeStruct((b, s, d), bf),
        jax.ShapeDtypeStruct((b, s, d), bf),
    ]
    out_specs = [
        pl.BlockSpec((1, N_HEADS, HEAD_PAD, tm), lambda i, j: (i, 0, 0, j)),
        pl.BlockSpec((1, N_HEADS, tm, HEAD_PAD), lambda i, j: (i, 0, j, 0)),
        pl.BlockSpec((1, N_HEADS, V_ROWS, tm), lambda i, j: (i, 0, 0, j)),
        pl.BlockSpec((1, tm, d), lambda i, j: (i, j, 0)),
        pl.BlockSpec((1, tm, d), lambda i, j: (i, j, 0)),
    ]
    in_specs = [
        pl.BlockSpec((1, tm, d), lambda i, j: (i, j, 0)),
        pl.BlockSpec((1, 3 * N_SUB, d), lambda i, j: (i, 0, 0)),
        _resident(norm_w.shape),
        pl.BlockSpec((1, ROPE_HALF, tm), lambda i, j: (i, 0, j)),
        pl.BlockSpec((1, ROPE_HALF, tm), lambda i, j: (i, 0, j)),
        _resident(wa_t.shape), _resident(wcg.shape), _resident(wuq_t.shape), _resident(wukv_t.shape),
        _resident(colw.shape), _resident(conv_w.shape), _resident(wb1.shape),
    ]
    return pl.pallas_call(
        _mix_kernel,
        grid=(b, s // tm),
        in_specs=in_specs,
        out_specs=out_specs,
        out_shape=out_shape,
        scratch_shapes=[pltpu.VMEM((8, CONV_WIDTH), jnp.float32)],
        compiler_params=pltpu.CompilerParams(
            dimension_semantics=("arbitrary", "arbitrary"), vmem_limit_bytes=VMEM_LIMIT),
        name="mixer_inputs",
    )(x, mod, norm_w, cos_t, sin_t, wa_t, wcg, wuq_t, wukv_t, colw, conv_w, wb1)


def _attn_kernel(q_ref, k_ref, v_ref, o_ref, m_ref, acc_ref):
    tq = q_ref.shape[3]
    tk = ATT_TK
    i = pl.program_id(2)

    def scores(start, n):
        return [jnp.dot(k_ref[0, hd, pl.ds(start, n), :], q_ref[0, hd],
                        preferred_element_type=jnp.float32) for hd in range(ATT_HG)]

    diag = pl.multiple_of(i * tq, tq)
    key = lax.broadcasted_iota(jnp.int32, (tq, tq), 0)
    qry = lax.broadcasted_iota(jnp.int32, (tq, tq), 1)
    visible = key <= qry
    for hd, s in enumerate(scores(diag, tq)):
        s = jnp.where(visible, s, NEG_BIG)
        m = jnp.max(s, axis=0, keepdims=True)
        p = jnp.exp2(s - m).astype(jnp.bfloat16)
        acc_ref[hd] = jnp.dot(v_ref[0, hd, :, pl.ds(diag, tq)], p, preferred_element_type=jnp.float32)
        m_ref[hd] = m

    def body(j, carry):
        start = pl.multiple_of(j * tk, tk)
        for hd, s in enumerate(scores(start, tk)):
            m_old = m_ref[hd]
            m_new = jnp.maximum(m_old, jnp.max(s, axis=0, keepdims=True))
            p = jnp.exp2(s - m_new).astype(jnp.bfloat16)
            alpha = jnp.exp2(m_old - m_new)
            pv = jnp.dot(v_ref[0, hd, :, pl.ds(start, tk)], p, preferred_element_type=jnp.float32)
            acc_ref[hd] = alpha * acc_ref[hd] + pv
            m_ref[hd] = m_new
        return carry

    lax.fori_loop(0, i * (tq // tk), body, 0)

    for hd in range(ATT_HG):
        acc = acc_ref[hd]
        o_ref[0, hd] = (acc[0:V_HEAD_DIM] / acc[V_HEAD_DIM:V_HEAD_DIM + 1]).astype(o_ref.dtype)


def _attention(q_t, k, v_t):
    b, nh, _, s = q_t.shape
    tq = ATT_TQ
    hg = ATT_HG
    return pl.pallas_call(
        _attn_kernel,
        grid=(b, nh // hg, s // tq),
        in_specs=[pl.BlockSpec((1, hg, HEAD_PAD, tq), lambda bi, g, i: (bi, g, 0, i)),
                  pl.BlockSpec((1, hg, s, HEAD_PAD), lambda bi, g, i: (bi, g, 0, 0)),
                  pl.BlockSpec((1, hg, V_ROWS, s), lambda bi, g, i: (bi, g, 0, 0))],
        out_specs=pl.BlockSpec((1, hg, V_HEAD_DIM, tq), lambda bi, g, i: (bi, g, 0, i)),
        out_shape=jax.ShapeDtypeStruct((b, nh, V_HEAD_DIM, s), jnp.bfloat16),
        scratch_shapes=[pltpu.VMEM((hg, 1, tq), jnp.float32),
                        pltpu.VMEM((hg, V_ROWS, tq), jnp.float32)],
        compiler_params=pltpu.CompilerParams(
            dimension_semantics=("arbitrary", "arbitrary", "arbitrary"), vmem_limit_bytes=VMEM_LIMIT),
        name="mla_attention",
    )(q_t, k, v_t)


def _post_kernel(x_ref, mod_ref, o_ref, g0_ref, mc_ref, wb0_ref, wout_ref, y_ref):
    o_t = o_ref[0]
    y_attn = lax.dot_general(o_t, wb0_ref[...], (((0,), (0,)), ((), ())),
                             preferred_element_type=jnp.float32)
    merged = g0_ref[0].astype(jnp.float32) * y_attn + mc_ref[0].astype(jnp.float32)
    y = jnp.dot(merged.astype(jnp.bfloat16), wout_ref[...], preferred_element_type=jnp.float32)
    y_ref[0] = x_ref[0] + mod_ref[0, 5:6, :] * y


def _post(x, mod, o_t, g0, mc, wb0, wout):
    b, s, d = x.shape
    tm = POST_TM
    rows = o_t.shape[1]
    return pl.pallas_call(
        _post_kernel,
        grid=(b, s // tm),
        in_specs=[pl.BlockSpec((1, tm, d), lambda i, j: (i, j, 0)),
                  pl.BlockSpec((1, 3 * N_SUB, d), lambda i, j: (i, 0, 0)),
                  pl.BlockSpec((1, rows, tm), lambda i, j: (i, 0, j)),
                  pl.BlockSpec((1, tm, d), lambda i, j: (i, j, 0)),
                  pl.BlockSpec((1, tm, d), lambda i, j: (i, j, 0)),
                  _resident(wb0.shape), _resident(wout.shape)],
        out_specs=pl.BlockSpec((1, tm, d), lambda i, j: (i, j, 0)),
        out_shape=jax.ShapeDtypeStruct(x.shape, jnp.float32),
        compiler_params=pltpu.CompilerParams(
            dimension_semantics=("arbitrary", "arbitrary"), vmem_limit_bytes=VMEM_LIMIT),
        name="merge_out",
    )(x, mod, o_t, g0, mc, wb0, wout)


def _bcast_cols(v):
    return jnp.broadcast_to(v[:, None], (v.shape[0], LANES))


def kernel(x, c, positions, w_ada, b_ada, norm_w, ffn_w13, ffn_w2, w_in, q_a_norm, w_uq, kv_a_norm,
           w_ukv, q_norm_nope, k_norm_nope, q_norm_rope, k_norm_rope, conv_w, w_branch, w_out):
    bf = jnp.bfloat16
    b, s, d = x.shape
    depth = w_ada.shape[0]
    cos_t, sin_t = _rope_tables(positions)
    for l in range(depth):
        mod = _modulation(c, w_ada[l], b_ada[l]).reshape(b, 3 * N_SUB, d)

        wi = w_in[l]
        wa_t = wi[:, :A_ROWS].T.astype(bf)
        wcg = wi[:, A_ROWS:].astype(bf)
        wuq = w_uq[l].reshape(Q_LORA_RANK, N_HEADS, QK_NOPE_DIM + QK_ROPE_DIM)
        wuq = jnp.pad(wuq, ((0, 0), (0, 0), (0, HEAD_PAD - QK_NOPE_DIM - QK_ROPE_DIM)))
        wuq_t = wuq.reshape(Q_LORA_RANK, N_HEADS * HEAD_PAD).T.astype(bf)
        wukv_t = w_ukv[l].T.astype(bf)
        colw = jnp.concatenate([_bcast_cols(q_a_norm[l]), _bcast_cols(kv_a_norm[l]),
                                _bcast_cols(q_norm_nope[l]), _bcast_cols(k_norm_nope[l]),
                                _bcast_cols(q_norm_rope[l]), _bcast_cols(k_norm_rope[l])], axis=0)

        x = _ffn(x, mod, norm_w[l], ffn_w13[l, 0].astype(bf), ffn_w2[l, 0].astype(bf), 0)
        q_t, k, v_t, g0, mc = _mixer_inputs(x, mod, norm_w[l], cos_t, sin_t, wa_t, wcg, wuq_t, wukv_t,
                                            colw, conv_w[l], w_branch[l, 1].astype(bf))
        o_t = _attention(q_t, k, v_t)
        x = _post(x, mod, o_t.reshape(b, N_HEADS * V_HEAD_DIM, s), g0, mc,
                  w_branch[l, 0].astype(bf), w_out[l].astype(bf))
        x = _ffn(x, mod, norm_w[l], ffn_w13[l, 1].astype(bf), ffn_w2[l, 1].astype(bf), 2)
    return x
```

```python
import functools
import math

import jax
import jax.numpy as jnp
from jax import lax
from jax.experimental import pallas as pl
from jax.experimental.pallas import tpu as pltpu

D_MODEL = 1024
N_HEADS = 8
QK_NOPE_DIM = 64
QK_ROPE_DIM = 32
V_HEAD_DIM = 64
Q_LORA_RANK = 256
KV_LORA_RANK = 128
CONV_WIDTH = 512
CONV_K = 3
D_FF = 2816
N_SUB = 3
ROPE_THETA = 10000.0
EPS = 1e-6

LANES = 128
HEAD_PAD = 128
ROPE_HALF = QK_ROPE_DIM // 2
V_ROWS = V_HEAD_DIM + 16
A_ROWS = Q_LORA_RANK + KV_LORA_RANK + QK_ROPE_DIM
CG_COLS = 3 * CONV_WIDTH + 2 * D_MODEL
VMEM_LIMIT = 56 * 1024 * 1024

FFN_TM = 512
MIX_TM = 512
POST_TM = 512
ATT_TQ = 512
ATT_TK = 512
ATT_HG = 4
ATT_UNROLL = 4
ATT_LAG = 2

Q_SCALE = (QK_NOPE_DIM + QK_ROPE_DIM) ** -0.5 * math.log2(math.e)
NEG_BIG = -1e30
SHIFT_BOUND_MAX = 50.0
ROUNDING_SLACK = 1.01


def _resident(shape):
    n = len(shape)
    return pl.BlockSpec(shape, lambda *_: (0,) * n, pipeline_mode=pl.Buffered(1))


def _modulated_norm(x, nw, shift, scale):
    ms = jnp.mean(x * x, axis=-1, keepdims=True)
    return x * lax.rsqrt(ms + EPS) * (nw * (1.0 + scale)) + shift


def _rope_kernel(pos_ref, invf_ref, cos_ref, sin_ref):
    ang = pos_ref[0].astype(jnp.float32) * invf_ref[...]
    cos_ref[0] = jnp.cos(ang)
    sin_ref[0] = jnp.sin(ang)


def _rope_tables(positions):
    b, s = positions.shape
    inv_freq = 1.0 / (ROPE_THETA ** (jnp.arange(0, QK_ROPE_DIM, 2, dtype=jnp.float32) / QK_ROPE_DIM))
    invf = jnp.broadcast_to(inv_freq[:, None], (ROPE_HALF, s))
    out = jax.ShapeDtypeStruct((b, ROPE_HALF, s), jnp.float32)
    return pl.pallas_call(
        _rope_kernel,
        grid=(b,),
        in_specs=[pl.BlockSpec((1, 1, s), lambda i: (i, 0, 0)),
                  pl.BlockSpec((ROPE_HALF, s), lambda i: (0, 0))],
        out_specs=[pl.BlockSpec((1, ROPE_HALF, s), lambda i: (i, 0, 0))] * 2,
        out_shape=[out, out],
        name="rope_tables",
    )(positions.reshape(b, 1, s), invf)


def _mod_kernel(c_ref, w_ref, b_ref, o_ref):
    c = c_ref[...]
    c_act = c * jax.nn.sigmoid(c)
    o_ref[...] = jnp.dot(c_act.astype(jnp.bfloat16), w_ref[...].astype(jnp.bfloat16),
                         preferred_element_type=jnp.float32) + b_ref[...]


def _modulation(c, w_ada, b_ada):
    b, d = c.shape
    n = w_ada.shape[1]
    tn = 1024
    return pl.pallas_call(
        _mod_kernel,
        grid=(n // tn,),
        in_specs=[pl.BlockSpec((b, d), lambda j: (0, 0)),
                  pl.BlockSpec((d, tn), lambda j: (0, j)),
                  pl.BlockSpec((1, tn), lambda j: (0, j))],
        out_specs=pl.BlockSpec((b, tn), lambda j: (0, j)),
        out_shape=jax.ShapeDtypeStruct((b, n), jnp.float32),
        name="adaln_mod",
    )(c, w_ada, b_ada.reshape(1, n))


def _ffn_kernel(sub, x_ref, mod_ref, nw_ref, w13_ref, w2_ref, o_ref):
    x = x_ref[0]
    shift = mod_ref[0, 3 * sub + 0:3 * sub + 1, :]
    scale = mod_ref[0, 3 * sub + 1:3 * sub + 2, :]
    gate = mod_ref[0, 3 * sub + 2:3 * sub + 3, :]
    h = _modulated_norm(x, nw_ref[sub:sub + 1, :], shift, scale).astype(jnp.bfloat16)
    gu = jnp.dot(h, w13_ref[...], preferred_element_type=jnp.float32)
    g = gu[:, :D_FF]
    u = gu[:, D_FF:]
    a = (g * jax.nn.sigmoid(g) * u).astype(jnp.bfloat16)
    y = jnp.dot(a, w2_ref[...], preferred_element_type=jnp.float32)
    o_ref[0] = x + (0.5 * gate) * y


def _ffn(x, mod, norm_w, w13, w2, sub):
    b, s, d = x.shape
    tm = FFN_TM
    return pl.pallas_call(
        functools.partial(_ffn_kernel, sub),
        grid=(b, s // tm),
        in_specs=[pl.BlockSpec((1, tm, d), lambda i, j: (i, j, 0)),
                  pl.BlockSpec((1, 3 * N_SUB, d), lambda i, j: (i, 0, 0)),
                  _resident(norm_w.shape),
                  _resident(w13.shape),
                  _resident(w2.shape)],
        out_specs=pl.BlockSpec((1, tm, d), lambda i, j: (i, j, 0)),
        out_shape=jax.ShapeDtypeStruct(x.shape, jnp.float32),
        compiler_params=pltpu.CompilerParams(
            dimension_semantics=("arbitrary", "arbitrary"), vmem_limit_bytes=VMEM_LIMIT),
        name=f"ffn{sub}",
    )(x, mod, norm_w, w13, w2)


def _rms_rows(x, w):
    ms = jnp.mean(x * x, axis=0, keepdims=True)
    return x * lax.rsqrt(ms + EPS) * w


def _rope_rows(y, cos, sin):
    y1 = y[:ROPE_HALF]
    y2 = y[ROPE_HALF:]
    return y1 * cos - y2 * sin, y2 * cos + y1 * sin


def _mix_kernel(x_ref, mod_ref, nw_ref, cos_ref, sin_ref, wa_ref, wcg_ref, wuq_ref, wukv_ref,
                colw_ref, convw_ref, wb1_ref,
                q_ref, k_ref, v_ref, g0_ref, mc_ref, tail_ref):
    tm = x_ref.shape[1]
    rep = tm // LANES
    j = pl.program_id(1)

    @pl.when(j == 0)
    def _():
        tail_ref[...] = jnp.zeros_like(tail_ref)

    x = x_ref[0]
    h = _modulated_norm(x, nw_ref[1:2, :], mod_ref[0, 3:4, :], mod_ref[0, 4:5, :]).astype(jnp.bfloat16)

    zt = lax.dot_general(wa_ref[...], h, (((1,), (1,)), ((), ())),
                         preferred_element_type=jnp.float32)
    cv = jnp.dot(h, wcg_ref[:, 0:3 * CONV_WIDTH], preferred_element_type=jnp.float32)

    def colw(lo, n):
        return jnp.tile(colw_ref[lo:lo + n, :], (1, rep))

    o_qa, o_kva = 0, Q_LORA_RANK
    o_qn = o_kva + KV_LORA_RANK
    o_kn = o_qn + QK_NOPE_DIM
    o_qr = o_kn + QK_NOPE_DIM
    o_kr = o_qr + QK_ROPE_DIM

    cq = _rms_rows(zt[0:Q_LORA_RANK], colw(o_qa, Q_LORA_RANK)).astype(jnp.bfloat16)
    ckv = _rms_rows(zt[Q_LORA_RANK:Q_LORA_RANK + KV_LORA_RANK], colw(o_kva, KV_LORA_RANK)).astype(jnp.bfloat16)
    kr_raw = zt[Q_LORA_RANK + KV_LORA_RANK:A_ROWS]
    cos = cos_ref[0]
    sin = sin_ref[0]
    kr1, kr2 = _rope_rows(_rms_rows(kr_raw, colw(o_kr, QK_ROPE_DIM)), cos, sin)

    qt = jnp.dot(wuq_ref[...], cq, preferred_element_type=jnp.float32)
    kvt = jnp.dot(wukv_ref[...], ckv, preferred_element_type=jnp.float32)
    gl = jnp.dot(h, wcg_ref[:, 3 * CONV_WIDTH:CG_COLS], preferred_element_type=jnp.float32)

    xv = cv[:, 0:CONV_WIDTH]
    gate_b = cv[:, CONV_WIDTH:2 * CONV_WIDTH]
    gate_c = cv[:, 2 * CONV_WIDTH:3 * CONV_WIDTH]
    u = gate_c * xv

    p1 = tail_ref[7:8, :]
    p2 = tail_ref[6:7, :]
    row = lax.broadcasted_iota(jnp.int32, u.shape, 0)
    r1 = jnp.where(row == 0, p1, pltpu.roll(u, 1, 0))
    r2 = jnp.where(row == 0, p2, jnp.where(row == 1, p1, pltpu.roll(u, 2, 0)))
    tail_ref[...] = u[tm - 8:tm, :]
    conv = convw_ref[0:1, :] * r2 + convw_ref[1:2, :] * r1 + convw_ref[2:3, :] * u
    o_conv = (gate_b * conv).astype(jnp.bfloat16)
    y_conv = jnp.dot(o_conv, wb1_ref[...], preferred_element_type=jnp.float32)
    g0_ref[0] = jax.nn.sigmoid(gl[:, 0:D_MODEL]).astype(jnp.bfloat16)
    mc_ref[0] = (jax.nn.sigmoid(gl[:, D_MODEL:2 * D_MODEL]) * y_conv).astype(jnp.bfloat16)

    wqn = colw(o_qn, QK_NOPE_DIM)
    wkn = colw(o_kn, QK_NOPE_DIM)
    wqr = colw(o_qr, QK_ROPE_DIM)
    zpad = jnp.zeros((HEAD_PAD - QK_NOPE_DIM - QK_ROPE_DIM, tm), jnp.float32)
    ones = jnp.ones((V_ROWS - V_HEAD_DIM, tm), jnp.bfloat16)
    for hd in range(N_HEADS):
        base = hd * HEAD_PAD
        qn = _rms_rows(qt[base:base + QK_NOPE_DIM], wqn)
        qr1, qr2 = _rope_rows(
            _rms_rows(qt[base + QK_NOPE_DIM:base + QK_NOPE_DIM + QK_ROPE_DIM], wqr), cos, sin)
        q_ref[0, hd, 0:QK_NOPE_DIM, :] = qn.astype(jnp.bfloat16)
        q_ref[0, hd, QK_NOPE_DIM:QK_NOPE_DIM + ROPE_HALF, :] = qr1.astype(jnp.bfloat16)
        q_ref[0, hd, QK_NOPE_DIM + ROPE_HALF:QK_NOPE_DIM + QK_ROPE_DIM, :] = qr2.astype(jnp.bfloat16)
        q_ref[0, hd, QK_NOPE_DIM + QK_ROPE_DIM:HEAD_PAD, :] = zpad.astype(jnp.bfloat16)

        kn = _rms_rows(kvt[base:base + QK_NOPE_DIM], wkn)
        kt = jnp.concatenate([kn, kr1, kr2, zpad], axis=0)
        k_ref[0, hd] = kt.T.astype(jnp.bfloat16)

        v_ref[0, hd, 0:V_HEAD_DIM, :] = kvt[base + QK_NOPE_DIM:base + HEAD_PAD].astype(jnp.bfloat16)
        v_ref[0, hd, V_HEAD_DIM:V_ROWS, :] = ones


def _mixer_inputs(x, mod, norm_w, cos_t, sin_t, wa_t, wcg, wuq_t, wukv_t, colw, conv_w, wb1):
    b, s, d = x.shape
    tm = MIX_TM
    bf = jnp.bfloat16
    out_shape = [
        jax.ShapeDtypeStruct((b, N_HEADS, HEAD_PAD, s), bf),
        jax.ShapeDtypeStruct((b, N_HEADS, s, HEAD_PAD), bf),
        jax.ShapeDtypeStruct((b, N_HEADS, V_ROWS, s), bf),
        jax.ShapeDtypeStruct((b, s, d), bf),
        jax.ShapeDtypeStruct((b, s, d), bf),
    ]
    out_specs = [
        pl.BlockSpec((1, N_HEADS, HEAD_PAD, tm), lambda i, j: (i, 0, 0, j)),
        pl.BlockSpec((1, N_HEADS, tm, HEAD_PAD), lambda i, j: (i, 0, j, 0)),
        pl.BlockSpec((1, N_HEADS, V_ROWS, tm), lambda i, j: (i, 0, 0, j)),
        pl.BlockSpec((1, tm, d), lambda i, j: (i, j, 0)),
        pl.BlockSpec((1, tm, d), lambda i, j: (i, j, 0)),
    ]
    in_specs = [
        pl.BlockSpec((1, tm, d), lambda i, j: (i, j, 0)),
        pl.BlockSpec((1, 3 * N_SUB, d), lambda i, j: (i, 0, 0)),
        _resident(norm_w.shape),
        pl.BlockSpec((1, ROPE_HALF, tm), lambda i, j: (i, 0, j)),
        pl.BlockSpec((1, ROPE_HALF, tm), lambda i, j: (i, 0, j)),
        _resident(wa_t.shape), _resident(wcg.shape), _resident(wuq_t.shape), _resident(wukv_t.shape),
        _resident(colw.shape), _resident(conv_w.shape), _resident(wb1.shape),
    ]
    return pl.pallas_call(
        _mix_kernel,
        grid=(b, s // tm),
        in_specs=in_specs,
        out_specs=out_specs,
        out_shape=out_shape,
        scratch_shapes=[pltpu.VMEM((8, CONV_WIDTH), jnp.float32)],
        compiler_params=pltpu.CompilerParams(
            dimension_semantics=("arbitrary", "arbitrary"), vmem_limit_bytes=VMEM_LIMIT),
        name="mixer_inputs",
    )(x, mod, norm_w, cos_t, sin_t, wa_t, wcg, wuq_t, wukv_t, colw, conv_w, wb1)


def _attn_kernel(shift_ref, q_ref, k_ref, v_ref, o_ref, m_ref, acc_ref):
    tq = q_ref.shape[3]
    tk = ATT_TK
    i = pl.program_id(2)
    diag = pl.multiple_of(i * tq, tq)
    n_chunks = i * (tq // tk)
    shift = shift_ref[0]

    key = lax.broadcasted_iota(jnp.int32, (tq, tq), 0)
    qry = lax.broadcasted_iota(jnp.int32, (tq, tq), 1)
    visible = key <= qry

    def shifted_chunks(chunks):
        items = [(c, hd) for c in range(len(chunks)) for hd in range(ATT_HG)]
        pending = {}
        out = [None] * ATT_HG
        for n in range(len(items) + ATT_LAG):
            if n < len(items):
                c, hd = items[n]
                start, size, masked = chunks[c]
                s = jnp.dot(k_ref[0, hd, pl.ds(start, size), :], q_ref[0, hd],
                            preferred_element_type=jnp.float32)
                pending[n] = jnp.where(visible, s, NEG_BIG) if masked else s
            if n >= ATT_LAG:
                c, hd = items[n - ATT_LAG]
                start, size, _ = chunks[c]
                p = jnp.exp2(pending.pop(n - ATT_LAG) - shift).astype(jnp.bfloat16)
                pv = jnp.dot(v_ref[0, hd, :, pl.ds(start, size)], p, preferred_element_type=jnp.float32)
                out[hd] = pv if out[hd] is None else out[hd] + pv
        return out

    @pl.when(shift <= SHIFT_BOUND_MAX)
    def _():
        for hd, pv in enumerate(shifted_chunks([(diag, tq, True)])):
            acc_ref[hd] = pv

        def accumulate(first, count):
            base = first * tk
            chunks = [(pl.multiple_of(base + u * tk, tk), tk, False) for u in range(count)]
            for hd, pv in enumerate(shifted_chunks(chunks)):
                acc_ref[hd] += pv

        n_groups = n_chunks // ATT_UNROLL

        def group_body(g, carry):
            accumulate(g * ATT_UNROLL, ATT_UNROLL)
            return carry

        lax.fori_loop(0, n_groups, group_body, 0)

        def rest_body(j, carry):
            accumulate(j, 1)
            return carry

        lax.fori_loop(n_groups * ATT_UNROLL, n_chunks, rest_body, 0)

    @pl.when(shift > SHIFT_BOUND_MAX)
    def _():
        def scores(start, n):
            return [jnp.dot(k_ref[0, hd, pl.ds(start, n), :], q_ref[0, hd],
                            preferred_element_type=jnp.float32) for hd in range(ATT_HG)]

        for hd, s in enumerate(scores(diag, tq)):
            s = jnp.where(visible, s, NEG_BIG)
            m = jnp.max(s, axis=0, keepdims=True)
            p = jnp.exp2(s - m).astype(jnp.bfloat16)
            acc_ref[hd] = jnp.dot(v_ref[0, hd, :, pl.ds(diag, tq)], p, preferred_element_type=jnp.float32)
            m_ref[hd] = m

        def body(j, carry):
            start = pl.multiple_of(j * tk, tk)
            for hd, s in enumerate(scores(start, tk)):
                m_old = m_ref[hd]
                m_new = jnp.maximum(m_old, jnp.max(s, axis=0, keepdims=True))
                p = jnp.exp2(s - m_new).astype(jnp.bfloat16)
                alpha = jnp.exp2(m_old - m_new)
                pv = jnp.dot(v_ref[0, hd, :, pl.ds(start, tk)], p, preferred_element_type=jnp.float32)
                acc_ref[hd] = alpha * acc_ref[hd] + pv
                m_ref[hd] = m_new
            return carry

        lax.fori_loop(0, n_chunks, body, 0)

    for hd in range(ATT_HG):
        acc = acc_ref[hd]
        o_ref[0, hd] = (acc[0:V_HEAD_DIM] / acc[V_HEAD_DIM:V_HEAD_DIM + 1]).astype(o_ref.dtype)


def _attention(shift, q_t, k, v_t):
    b, nh, _, s = q_t.shape
    tq = ATT_TQ
    hg = ATT_HG
    return pl.pallas_call(
        _attn_kernel,
        grid=(b, nh // hg, s // tq),
        in_specs=[pl.BlockSpec(memory_space=pltpu.SMEM),
                  pl.BlockSpec((1, hg, HEAD_PAD, tq), lambda bi, g, i: (bi, g, 0, i)),
                  pl.BlockSpec((1, hg, s, HEAD_PAD), lambda bi, g, i: (bi, g, 0, 0)),
                  pl.BlockSpec((1, hg, V_ROWS, s), lambda bi, g, i: (bi, g, 0, 0))],
        out_specs=pl.BlockSpec((1, hg, V_HEAD_DIM, tq), lambda bi, g, i: (bi, g, 0, i)),
        out_shape=jax.ShapeDtypeStruct((b, nh, V_HEAD_DIM, s), jnp.bfloat16),
        scratch_shapes=[pltpu.VMEM((hg, 1, tq), jnp.float32),
                        pltpu.VMEM((hg, V_ROWS, tq), jnp.float32)],
        compiler_params=pltpu.CompilerParams(
            dimension_semantics=("arbitrary", "arbitrary", "arbitrary"), vmem_limit_bytes=VMEM_LIMIT),
        name="mla_attention",
    )(shift, q_t, k, v_t)


def _post_kernel(x_ref, mod_ref, o_ref, g0_ref, mc_ref, wb0_ref, wout_ref, y_ref):
    o_t = o_ref[0]
    y_attn = lax.dot_general(o_t, wb0_ref[...], (((0,), (0,)), ((), ())),
                             preferred_element_type=jnp.float32)
    merged = g0_ref[0].astype(jnp.float32) * y_attn + mc_ref[0].astype(jnp.float32)
    y = jnp.dot(merged.astype(jnp.bfloat16), wout_ref[...], preferred_element_type=jnp.float32)
    y_ref[0] = x_ref[0] + mod_ref[0, 5:6, :] * y


def _post(x, mod, o_t, g0, mc, wb0, wout):
    b, s, d = x.shape
    tm = POST_TM
    rows = o_t.shape[1]
    return pl.pallas_call(
        _post_kernel,
        grid=(b, s // tm),
        in_specs=[pl.BlockSpec((1, tm, d), lambda i, j: (i, j, 0)),
                  pl.BlockSpec((1, 3 * N_SUB, d), lambda i, j: (i, 0, 0)),
                  pl.BlockSpec((1, rows, tm), lambda i, j: (i, 0, j)),
                  pl.BlockSpec((1, tm, d), lambda i, j: (i, j, 0)),
                  pl.BlockSpec((1, tm, d), lambda i, j: (i, j, 0)),
                  _resident(wb0.shape), _resident(wout.shape)],
        out_specs=pl.BlockSpec((1, tm, d), lambda i, j: (i, j, 0)),
        out_shape=jax.ShapeDtypeStruct(x.shape, jnp.float32),
        compiler_params=pltpu.CompilerParams(
            dimension_semantics=("arbitrary", "arbitrary"), vmem_limit_bytes=VMEM_LIMIT),
        name="merge_out",
    )(x, mod, o_t, g0, mc, wb0, wout)


def _score_bound(wqn, wqr, wkn, wkr):
    def sq(w_nope, w_rope):
        return QK_NOPE_DIM * jnp.max(jnp.square(w_nope)) + QK_ROPE_DIM * jnp.max(jnp.square(w_rope))
    bound = Q_SCALE * jnp.sqrt(sq(wqn, wqr) * sq(wkn, wkr)) * ROUNDING_SLACK
    return bound.reshape(1).astype(jnp.float32)


def _bcast_cols(v):
    return jnp.broadcast_to(v[:, None], (v.shape[0], LANES))


def kernel(x, c, positions, w_ada, b_ada, norm_w, ffn_w13, ffn_w2, w_in, q_a_norm, w_uq, kv_a_norm,
           w_ukv, q_norm_nope, k_norm_nope, q_norm_rope, k_norm_rope, conv_w, w_branch, w_out):
    bf = jnp.bfloat16
    b, s, d = x.shape
    depth = w_ada.shape[0]
    cos_t, sin_t = _rope_tables(positions)
    for l in range(depth):
        mod = _modulation(c, w_ada[l], b_ada[l]).reshape(b, 3 * N_SUB, d)

        wi = w_in[l]
        wa_t = wi[:, :A_ROWS].T.astype(bf)
        wcg = wi[:, A_ROWS:].astype(bf)
        wuq = w_uq[l].reshape(Q_LORA_RANK, N_HEADS, QK_NOPE_DIM + QK_ROPE_DIM)
        wuq = jnp.pad(wuq, ((0, 0), (0, 0), (0, HEAD_PAD - QK_NOPE_DIM - QK_ROPE_DIM)))
        wuq_t = wuq.reshape(Q_LORA_RANK, N_HEADS * HEAD_PAD).T.astype(bf)
        wukv_t = w_ukv[l].T.astype(bf)
        colw = jnp.concatenate([_bcast_cols(q_a_norm[l]), _bcast_cols(kv_a_norm[l]),
                                _bcast_cols(q_norm_nope[l] * Q_SCALE), _bcast_cols(k_norm_nope[l]),
                                _bcast_cols(q_norm_rope[l] * Q_SCALE), _bcast_cols(k_norm_rope[l])], axis=0)
        shift = _score_bound(q_norm_nope[l], q_norm_rope[l], k_norm_nope[l], k_norm_rope[l])

        x = _ffn(x, mod, norm_w[l], ffn_w13[l, 0].astype(bf), ffn_w2[l, 0].astype(bf), 0)
        q_t, k, v_t, g0, mc = _mixer_inputs(x, mod, norm_w[l], cos_t, sin_t, wa_t, wcg, wuq_t, wukv_t,
                                            colw, conv_w[l], w_branch[l, 1].astype(bf))
        o_t = _attention(shift, q_t, k, v_t)
        x = _post(x, mod, o_t.reshape(b, N_HEADS * V_HEAD_DIM, s), g0, mc,
                  w_branch[l, 0].astype(bf), w_out[l].astype(bf))
        x = _ffn(x, mod, norm_w[l], ffn_w13[l, 1].astype(bf), ffn_w2[l, 1].astype(bf), 2)
    return x
```

```python
import functools
import math

import jax
import jax.numpy as jnp
from jax import lax
from jax.experimental import pallas as pl
from jax.experimental.pallas import tpu as pltpu

D_MODEL = 1024
N_HEADS = 8
QK_NOPE_DIM = 64
QK_ROPE_DIM = 32
V_HEAD_DIM = 64
Q_LORA_RANK = 256
KV_LORA_RANK = 128
CONV_WIDTH = 512
CONV_K = 3
D_FF = 2816
N_SUB = 3
ROPE_THETA = 10000.0
EPS = 1e-6

LANES = 128
HEAD_PAD = 128
ROPE_HALF = QK_ROPE_DIM // 2
V_ROWS = V_HEAD_DIM + 16
A_ROWS = Q_LORA_RANK + KV_LORA_RANK + QK_ROPE_DIM
CG_COLS = 3 * CONV_WIDTH + 2 * D_MODEL
VMEM_LIMIT = 56 * 1024 * 1024

FFN_TM = 512
FFN_SPLIT = 2
MIX_TM = 512
POST_TM = 512
ATT_TQ = 512
ATT_TK = 512
ATT_HG = 4
ATT_UNROLL = 4
ATT_LAG = 2

Q_SCALE = (QK_NOPE_DIM + QK_ROPE_DIM) ** -0.5 * math.log2(math.e)
NEG_BIG = -1e30
SHIFT_BOUND_MAX = 50.0
ROUNDING_SLACK = 1.01


def _resident(shape):
    n = len(shape)
    return pl.BlockSpec(shape, lambda *_: (0,) * n, pipeline_mode=pl.Buffered(1))


def _modulated_norm(x, nw, shift, scale):
    ms = jnp.mean(x * x, axis=-1, keepdims=True)
    return x * lax.rsqrt(ms + EPS) * (nw * (1.0 + scale)) + shift


def _rope_kernel(pos_ref, invf_ref, cos_ref, sin_ref):
    ang = pos_ref[0].astype(jnp.float32) * invf_ref[...]
    cos_ref[0] = jnp.cos(ang)
    sin_ref[0] = jnp.sin(ang)


def _rope_tables(positions):
    b, s = positions.shape
    inv_freq = 1.0 / (ROPE_THETA ** (jnp.arange(0, QK_ROPE_DIM, 2, dtype=jnp.float32) / QK_ROPE_DIM))
    invf = jnp.broadcast_to(inv_freq[:, None], (ROPE_HALF, s))
    out = jax.ShapeDtypeStruct((b, ROPE_HALF, s), jnp.float32)
    return pl.pallas_call(
        _rope_kernel,
        grid=(b,),
        in_specs=[pl.BlockSpec((1, 1, s), lambda i: (i, 0, 0)),
                  pl.BlockSpec((ROPE_HALF, s), lambda i: (0, 0))],
        out_specs=[pl.BlockSpec((1, ROPE_HALF, s), lambda i: (i, 0, 0))] * 2,
        out_shape=[out, out],
        name="rope_tables",
    )(positions.reshape(b, 1, s), invf)


def _mod_kernel(c_ref, w_ref, b_ref, o_ref):
    c = c_ref[...]
    c_act = c * jax.nn.sigmoid(c)
    o_ref[...] = jnp.dot(c_act.astype(jnp.bfloat16), w_ref[...].astype(jnp.bfloat16),
                         preferred_element_type=jnp.float32) + b_ref[...]


def _modulation(c, w_ada, b_ada):
    b, d = c.shape
    n = w_ada.shape[1]
    tn = 1024
    return pl.pallas_call(
        _mod_kernel,
        grid=(n // tn,),
        in_specs=[pl.BlockSpec((b, d), lambda j: (0, 0)),
                  pl.BlockSpec((d, tn), lambda j: (0, j)),
                  pl.BlockSpec((1, tn), lambda j: (0, j))],
        out_specs=pl.BlockSpec((b, tn), lambda j: (0, j)),
        out_shape=jax.ShapeDtypeStruct((b, n), jnp.float32),
        name="adaln_mod",
    )(c, w_ada, b_ada.reshape(1, n))


def _ffn_rows(xs, sub, mod_ref, nw_ref, w13_ref, w2_ref, o_ref):
    shift = mod_ref[0, 3 * sub + 0:3 * sub + 1, :]
    scale = mod_ref[0, 3 * sub + 1:3 * sub + 2, :]
    gate = 0.5 * mod_ref[0, 3 * sub + 2:3 * sub + 3, :]
    nw = nw_ref[sub:sub + 1, :]
    rows = xs[0].shape[0]
    hs = [_modulated_norm(x, nw, shift, scale).astype(jnp.bfloat16) for x in xs]
    gus = [jnp.dot(h, w13_ref[...], preferred_element_type=jnp.float32) for h in hs]
    for r, x in enumerate(xs):
        g = gus[r][:, :D_FF]
        u = gus[r][:, D_FF:]
        a = (g * jax.nn.sigmoid(g) * u).astype(jnp.bfloat16)
        y = jnp.dot(a, w2_ref[...], preferred_element_type=jnp.float32)
        o_ref[0, r * rows:(r + 1) * rows, :] = x + gate * y


def _ffn_kernel(sub, x_ref, mod_ref, nw_ref, w13_ref, w2_ref, o_ref):
    rows = x_ref.shape[1] // FFN_SPLIT
    xs = [x_ref[0, r * rows:(r + 1) * rows, :] for r in range(FFN_SPLIT)]
    _ffn_rows(xs, sub, mod_ref, nw_ref, w13_ref, w2_ref, o_ref)


def _ffn(x, mod, norm_w, w13, w2, sub):
    b, s, d = x.shape
    tm = FFN_TM
    return pl.pallas_call(
        functools.partial(_ffn_kernel, sub),
        grid=(b, s // tm),
        in_specs=[pl.BlockSpec((1, tm, d), lambda i, j: (i, j, 0)),
                  pl.BlockSpec((1, 3 * N_SUB, d), lambda i, j: (i, 0, 0)),
                  _resident(norm_w.shape),
                  _resident(w13.shape),
                  _resident(w2.shape)],
        out_specs=pl.BlockSpec((1, tm, d), lambda i, j: (i, j, 0)),
        out_shape=jax.ShapeDtypeStruct(x.shape, jnp.float32),
        compiler_params=pltpu.CompilerParams(
            dimension_semantics=("arbitrary", "arbitrary"), vmem_limit_bytes=VMEM_LIMIT),
        name=f"ffn{sub}",
    )(x, mod, norm_w, w13, w2)


def _rms_rows(x, w):
    ms = jnp.mean(x * x, axis=0, keepdims=True)
    return x * lax.rsqrt(ms + EPS) * w


def _rope_rows(y, cos, sin):
    y1 = y[:ROPE_HALF]
    y2 = y[ROPE_HALF:]
    return y1 * cos - y2 * sin, y2 * cos + y1 * sin


def _mix_kernel(x_ref, mod_ref, nw_ref, cos_ref, sin_ref, wa_ref, wcg_ref, wuq_ref, wukv_ref,
                colw_ref, convw_ref, wb1_ref,
                q_ref, k_ref, v_ref, g0_ref, mc_ref, tail_ref):
    tm = x_ref.shape[1]
    rep = tm // LANES
    j = pl.program_id(1)

    @pl.when(j == 0)
    def _():
        tail_ref[...] = jnp.zeros_like(tail_ref)

    x = x_ref[0]
    h = _modulated_norm(x, nw_ref[1:2, :], mod_ref[0, 3:4, :], mod_ref[0, 4:5, :]).astype(jnp.bfloat16)

    zt = lax.dot_general(wa_ref[...], h, (((1,), (1,)), ((), ())),
                         preferred_element_type=jnp.float32)
    cv = jnp.dot(h, wcg_ref[:, 0:3 * CONV_WIDTH], preferred_element_type=jnp.float32)

    def colw(lo, n):
        return jnp.tile(colw_ref[lo:lo + n, :], (1, rep))

    o_qa, o_kva = 0, Q_LORA_RANK
    o_qn = o_kva + KV_LORA_RANK
    o_kn = o_qn + QK_NOPE_DIM
    o_qr = o_kn + QK_NOPE_DIM
    o_kr = o_qr + QK_ROPE_DIM

    cq = _rms_rows(zt[0:Q_LORA_RANK], colw(o_qa, Q_LORA_RANK)).astype(jnp.bfloat16)
    ckv = _rms_rows(zt[Q_LORA_RANK:Q_LORA_RANK + KV_LORA_RANK], colw(o_kva, KV_LORA_RANK)).astype(jnp.bfloat16)
    kr_raw = zt[Q_LORA_RANK + KV_LORA_RANK:A_ROWS]
    cos = cos_ref[0]
    sin = sin_ref[0]
    kr1, kr2 = _rope_rows(_rms_rows(kr_raw, colw(o_kr, QK_ROPE_DIM)), cos, sin)

    qt = jnp.dot(wuq_ref[...], cq, preferred_element_type=jnp.float32)
    kvt = jnp.dot(wukv_ref[...], ckv, preferred_element_type=jnp.float32)
    gl = jnp.dot(h, wcg_ref[:, 3 * CONV_WIDTH:CG_COLS], preferred_element_type=jnp.float32)

    xv = cv[:, 0:CONV_WIDTH]
    gate_b = cv[:, CONV_WIDTH:2 * CONV_WIDTH]
    gate_c = cv[:, 2 * CONV_WIDTH:3 * CONV_WIDTH]
    u = gate_c * xv

    p1 = tail_ref[7:8, :]
    p2 = tail_ref[6:7, :]
    row = lax.broadcasted_iota(jnp.int32, u.shape, 0)
    r1 = jnp.where(row == 0, p1, pltpu.roll(u, 1, 0))
    r2 = jnp.where(row == 0, p2, jnp.where(row == 1, p1, pltpu.roll(u, 2, 0)))
    tail_ref[...] = u[tm - 8:tm, :]
    conv = convw_ref[0:1, :] * r2 + convw_ref[1:2, :] * r1 + convw_ref[2:3, :] * u
    o_conv = (gate_b * conv).astype(jnp.bfloat16)
    y_conv = jnp.dot(o_conv, wb1_ref[...], preferred_element_type=jnp.float32)
    g0_ref[0] = jax.nn.sigmoid(gl[:, 0:D_MODEL]).astype(jnp.bfloat16)
    mc_ref[0] = (jax.nn.sigmoid(gl[:, D_MODEL:2 * D_MODEL]) * y_conv).astype(jnp.bfloat16)

    wqn = colw(o_qn, QK_NOPE_DIM)
    wkn = colw(o_kn, QK_NOPE_DIM)
    wqr = colw(o_qr, QK_ROPE_DIM)
    zpad = jnp.zeros((HEAD_PAD - QK_NOPE_DIM - QK_ROPE_DIM, tm), jnp.float32)
    ones = jnp.ones((V_ROWS - V_HEAD_DIM, tm), jnp.bfloat16)
    for hd in range(N_HEADS):
        base = hd * HEAD_PAD
        qn = _rms_rows(qt[base:base + QK_NOPE_DIM], wqn)
        qr1, qr2 = _rope_rows(
            _rms_rows(qt[base + QK_NOPE_DIM:base + QK_NOPE_DIM + QK_ROPE_DIM], wqr), cos, sin)
        q_ref[0, hd, 0:QK_NOPE_DIM, :] = qn.astype(jnp.bfloat16)
        q_ref[0, hd, QK_NOPE_DIM:QK_NOPE_DIM + ROPE_HALF, :] = qr1.astype(jnp.bfloat16)
        q_ref[0, hd, QK_NOPE_DIM + ROPE_HALF:QK_NOPE_DIM + QK_ROPE_DIM, :] = qr2.astype(jnp.bfloat16)
        q_ref[0, hd, QK_NOPE_DIM + QK_ROPE_DIM:HEAD_PAD, :] = zpad.astype(jnp.bfloat16)

        kn = _rms_rows(kvt[base:base + QK_NOPE_DIM], wkn)
        kt = jnp.concatenate([kn, kr1, kr2, zpad], axis=0)
        k_ref[0, hd] = kt.T.astype(jnp.bfloat16)

        v_ref[0, hd, 0:V_HEAD_DIM, :] = kvt[base + QK_NOPE_DIM:base + HEAD_PAD].astype(jnp.bfloat16)
        v_ref[0, hd, V_HEAD_DIM:V_ROWS, :] = ones


def _mixer_inputs(x, mod, norm_w, cos_t, sin_t, wa_t, wcg, wuq_t, wukv_t, colw, conv_w, wb1):
    b, s, d = x.shape
    tm = MIX_TM
    bf = jnp.bfloat16
    out_shape = [
        jax.ShapeDtypeStruct((b, N_HEADS, HEAD_PAD, s), bf),
        jax.ShapeDtypeStruct((b, N_HEADS, s, HEAD_PAD), bf),
        jax.ShapeDtypeStruct((b, N_HEADS, V_ROWS, s), bf),
        jax.ShapeDtypeStruct((b, s, d), bf),
        jax.ShapeDtypeStruct((b, s, d), bf),
    ]
    out_specs = [
        pl.BlockSpec((1, N_HEADS, HEAD_PAD, tm), lambda i, j: (i, 0, 0, j)),
        pl.BlockSpec((1, N_HEADS, tm, HEAD_PAD), lambda i, j: (i, 0, j, 0)),
        pl.BlockSpec((1, N_HEADS, V_ROWS, tm), lambda i, j: (i, 0, 0, j)),
        pl.BlockSpec((1, tm, d), lambda i, j: (i, j, 0)),
        pl.BlockSpec((1, tm, d), lambda i, j: (i, j, 0)),
    ]
    in_specs = [
        pl.BlockSpec((1, tm, d), lambda i, j: (i, j, 0)),
        pl.BlockSpec((1, 3 * N_SUB, d), lambda i, j: (i, 0, 0)),
        _resident(norm_w.shape),
        pl.BlockSpec((1, ROPE_HALF, tm), lambda i, j: (i, 0, j)),
        pl.BlockSpec((1, ROPE_HALF, tm), lambda i, j: (i, 0, j)),
        _resident(wa_t.shape), _resident(wcg.shape), _resident(wuq_t.shape), _resident(wukv_t.shape),
        _resident(colw.shape), _resident(conv_w.shape), _resident(wb1.shape),
    ]
    return pl.pallas_call(
        _mix_kernel,
        grid=(b, s // tm),
        in_specs=in_specs,
        out_specs=out_specs,
        out_shape=out_shape,
        scratch_shapes=[pltpu.VMEM((8, CONV_WIDTH), jnp.float32)],
        compiler_params=pltpu.CompilerParams(
            dimension_semantics=("arbitrary", "arbitrary"), vmem_limit_bytes=VMEM_LIMIT),
        name="mixer_inputs",
    )(x, mod, norm_w, cos_t, sin_t, wa_t, wcg, wuq_t, wukv_t, colw, conv_w, wb1)


def _attn_kernel(shift_ref, q_ref, k_ref, v_ref, o_ref, m_ref, acc_ref):
    tq = q_ref.shape[3]
    tk = ATT_TK
    i = pl.program_id(2)
    diag = pl.multiple_of(i * tq, tq)
    n_chunks = i * (tq // tk)
    shift = shift_ref[0]

    key = lax.broadcasted_iota(jnp.int32, (tq, tq), 0)
    qry = lax.broadcasted_iota(jnp.int32, (tq, tq), 1)
    visible = key <= qry

    def shifted_chunks(chunks):
        items = [(c, hd) for c in range(len(chunks)) for hd in range(ATT_HG)]
        pending = {}
        out = [None] * ATT_HG
        for n in range(len(items) + ATT_LAG):
            if n < len(items):
                c, hd = items[n]
                start, size, masked = chunks[c]
                s = jnp.dot(k_ref[0, hd, pl.ds(start, size), :], q_ref[0, hd],
                            preferred_element_type=jnp.float32)
                pending[n] = jnp.where(visible, s, NEG_BIG) if masked else s
            if n >= ATT_LAG:
                c, hd = items[n - ATT_LAG]
                start, size, _ = chunks[c]
                p = jnp.exp2(pending.pop(n - ATT_LAG) - shift).astype(jnp.bfloat16)
                pv = jnp.dot(v_ref[0, hd, :, pl.ds(start, size)], p, preferred_element_type=jnp.float32)
                out[hd] = pv if out[hd] is None else out[hd] + pv
        return out

    @pl.when(shift <= SHIFT_BOUND_MAX)
    def _():
        for hd, pv in enumerate(shifted_chunks([(diag, tq, True)])):
            acc_ref[hd] = pv

        def accumulate(first, count):
            base = first * tk
            chunks = [(pl.multiple_of(base + u * tk, tk), tk, False) for u in range(count)]
            for hd, pv in enumerate(shifted_chunks(chunks)):
                acc_ref[hd] += pv

        n_groups = n_chunks // ATT_UNROLL

        def group_body(g, carry):
            accumulate(g * ATT_UNROLL, ATT_UNROLL)
            return carry

        lax.fori_loop(0, n_groups, group_body, 0)

        def rest_body(j, carry):
            accumulate(j, 1)
            return carry

        lax.fori_loop(n_groups * ATT_UNROLL, n_chunks, rest_body, 0)

    @pl.when(shift > SHIFT_BOUND_MAX)
    def _():
        def scores(start, n):
            return [jnp.dot(k_ref[0, hd, pl.ds(start, n), :], q_ref[0, hd],
                            preferred_element_type=jnp.float32) for hd in range(ATT_HG)]

        for hd, s in enumerate(scores(diag, tq)):
            s = jnp.where(visible, s, NEG_BIG)
            m = jnp.max(s, axis=0, keepdims=True)
            p = jnp.exp2(s - m).astype(jnp.bfloat16)
            acc_ref[hd] = jnp.dot(v_ref[0, hd, :, pl.ds(diag, tq)], p, preferred_element_type=jnp.float32)
            m_ref[hd] = m

        def body(j, carry):
            start = pl.multiple_of(j * tk, tk)
            for hd, s in enumerate(scores(start, tk)):
                m_old = m_ref[hd]
                m_new = jnp.maximum(m_old, jnp.max(s, axis=0, keepdims=True))
                p = jnp.exp2(s - m_new).astype(jnp.bfloat16)
                alpha = jnp.exp2(m_old - m_new)
                pv = jnp.dot(v_ref[0, hd, :, pl.ds(start, tk)], p, preferred_element_type=jnp.float32)
                acc_ref[hd] = alpha * acc_ref[hd] + pv
                m_ref[hd] = m_new
            return carry

        lax.fori_loop(0, n_chunks, body, 0)

    for hd in range(ATT_HG):
        acc = acc_ref[hd]
        o_ref[0, hd] = (acc[0:V_HEAD_DIM] / acc[V_HEAD_DIM:V_HEAD_DIM + 1]).astype(o_ref.dtype)


def _attention(shift, q_t, k, v_t):
    b, nh, _, s = q_t.shape
    tq = ATT_TQ
    hg = ATT_HG
    return pl.pallas_call(
        _attn_kernel,
        grid=(b, nh // hg, s // tq),
        in_specs=[pl.BlockSpec(memory_space=pltpu.SMEM),
                  pl.BlockSpec((1, hg, HEAD_PAD, tq), lambda bi, g, i: (bi, g, 0, i)),
                  pl.BlockSpec((1, hg, s, HEAD_PAD), lambda bi, g, i: (bi, g, 0, 0)),
                  pl.BlockSpec((1, hg, V_ROWS, s), lambda bi, g, i: (bi, g, 0, 0))],
        out_specs=pl.BlockSpec((1, hg, V_HEAD_DIM, tq), lambda bi, g, i: (bi, g, 0, i)),
        out_shape=jax.ShapeDtypeStruct((b, nh, V_HEAD_DIM, s), jnp.bfloat16),
        scratch_shapes=[pltpu.VMEM((hg, 1, tq), jnp.float32),
                        pltpu.VMEM((hg, V_ROWS, tq), jnp.float32)],
        compiler_params=pltpu.CompilerParams(
            dimension_semantics=("arbitrary", "arbitrary", "arbitrary"), vmem_limit_bytes=VMEM_LIMIT),
        name="mla_attention",
    )(shift, q_t, k, v_t)


def _post_kernel(x_ref, mod_ref, nw_ref, o_ref, g0_ref, mc_ref, wb0_ref, wout_ref, w13_ref, w2_ref, y_ref):
    rows = x_ref.shape[1] // FFN_SPLIT
    gate = mod_ref[0, 5:6, :]
    xs = []
    for r in range(FFN_SPLIT):
        sl = slice(r * rows, (r + 1) * rows)
        y_attn = lax.dot_general(o_ref[0, :, sl], wb0_ref[...], (((0,), (0,)), ((), ())),
                                 preferred_element_type=jnp.float32)
        merged = g0_ref[0, sl, :].astype(jnp.float32) * y_attn + mc_ref[0, sl, :].astype(jnp.float32)
        y = jnp.dot(merged.astype(jnp.bfloat16), wout_ref[...], preferred_element_type=jnp.float32)
        xs.append(x_ref[0, sl, :] + gate * y)
    _ffn_rows(xs, 2, mod_ref, nw_ref, w13_ref, w2_ref, y_ref)


def _post(x, mod, norm_w, o_t, g0, mc, wb0, wout, w13, w2):
    b, s, d = x.shape
    tm = POST_TM
    rows = o_t.shape[1]
    return pl.pallas_call(
        _post_kernel,
        grid=(b, s // tm),
        in_specs=[pl.BlockSpec((1, tm, d), lambda i, j: (i, j, 0)),
                  pl.BlockSpec((1, 3 * N_SUB, d), lambda i, j: (i, 0, 0)),
                  _resident(norm_w.shape),
                  pl.BlockSpec((1, rows, tm), lambda i, j: (i, 0, j)),
                  pl.BlockSpec((1, tm, d), lambda i, j: (i, j, 0)),
                  pl.BlockSpec((1, tm, d), lambda i, j: (i, j, 0)),
                  _resident(wb0.shape), _resident(wout.shape), _resident(w13.shape), _resident(w2.shape)],
        out_specs=pl.BlockSpec((1, tm, d), lambda i, j: (i, j, 0)),
        out_shape=jax.ShapeDtypeStruct(x.shape, jnp.float32),
        compiler_params=pltpu.CompilerParams(
            dimension_semantics=("arbitrary", "arbitrary"), vmem_limit_bytes=VMEM_LIMIT),
        name="merge_out_ffn2",
    )(x, mod, norm_w, o_t, g0, mc, wb0, wout, w13, w2)


def _score_bound(wqn, wqr, wkn, wkr):
    def sq(w_nope, w_rope):
        return QK_NOPE_DIM * jnp.max(jnp.square(w_nope)) + QK_ROPE_DIM * jnp.max(jnp.square(w_rope))
    bound = Q_SCALE * jnp.sqrt(sq(wqn, wqr) * sq(wkn, wkr)) * ROUNDING_SLACK
    return bound.reshape(1).astype(jnp.float32)


def _bcast_cols(v):
    return jnp.broadcast_to(v[:, None], (v.shape[0], LANES))


def kernel(x, c, positions, w_ada, b_ada, norm_w, ffn_w13, ffn_w2, w_in, q_a_norm, w_uq, kv_a_norm,
           w_ukv, q_norm_nope, k_norm_nope, q_norm_rope, k_norm_rope, conv_w, w_branch, w_out):
    bf = jnp.bfloat16
    b, s, d = x.shape
    depth = w_ada.shape[0]
    cos_t, sin_t = _rope_tables(positions)
    for l in range(depth):
        mod = _modulation(c, w_ada[l], b_ada[l]).reshape(b, 3 * N_SUB, d)

        wi = w_in[l]
        wa_t = wi[:, :A_ROWS].T.astype(bf)
        wcg = wi[:, A_ROWS:].astype(bf)
        wuq = w_uq[l].reshape(Q_LORA_RANK, N_HEADS, QK_NOPE_DIM + QK_ROPE_DIM)
        wuq = jnp.pad(wuq, ((0, 0), (0, 0), (0, HEAD_PAD - QK_NOPE_DIM - QK_ROPE_DIM)))
        wuq_t = wuq.reshape(Q_LORA_RANK, N_HEADS * HEAD_PAD).T.astype(bf)
        wukv_t = w_ukv[l].T.astype(bf)
        colw = jnp.concatenate([_bcast_cols(q_a_norm[l]), _bcast_cols(kv_a_norm[l]),
                                _bcast_cols(q_norm_nope[l] * Q_SCALE), _bcast_cols(k_norm_nope[l]),
                                _bcast_cols(q_norm_rope[l] * Q_SCALE), _bcast_cols(k_norm_rope[l])], axis=0)
        shift = _score_bound(q_norm_nope[l], q_norm_rope[l], k_norm_nope[l], k_norm_rope[l])

        x = _ffn(x, mod, norm_w[l], ffn_w13[l, 0].astype(bf), ffn_w2[l, 0].astype(bf), 0)
        q_t, k, v_t, g0, mc = _mixer_inputs(x, mod, norm_w[l], cos_t, sin_t, wa_t, wcg, wuq_t, wukv_t,
                                            colw, conv_w[l], w_branch[l, 1].astype(bf))
        o_t = _attention(shift, q_t, k, v_t)
        x = _post(x, mod, norm_w[l], o_t.reshape(b, N_HEADS * V_HEAD_DIM, s), g0, mc,
                  w_branch[l, 0].astype(bf), w_out[l].astype(bf),
                  ffn_w13[l, 1].astype(bf), ffn_w2[l, 1].astype(bf))
    return x
```

```python
import functools
import math

import jax
import jax.numpy as jnp
from jax import lax
from jax.experimental import pallas as pl
from jax.experimental.pallas import tpu as pltpu

D_MODEL = 1024
N_HEADS = 8
QK_NOPE_DIM = 64
QK_ROPE_DIM = 32
V_HEAD_DIM = 64
Q_LORA_RANK = 256
KV_LORA_RANK = 128
CONV_WIDTH = 512
CONV_K = 3
D_FF = 2816
N_SUB = 3
ROPE_THETA = 10000.0
EPS = 1e-6

LANES = 128
HEAD_PAD = 128
ROPE_HALF = QK_ROPE_DIM // 2
V_ROWS = V_HEAD_DIM + 16
A_ROWS = Q_LORA_RANK + KV_LORA_RANK + QK_ROPE_DIM
CG_COLS = 3 * CONV_WIDTH + 2 * D_MODEL
VMEM_LIMIT = 56 * 1024 * 1024

FFN_TM = 512
FFN_SPLIT = 2
MIX_TM = 512
POST_TM = 512
ATT_TQ = 512
ATT_TK = 512
ATT_HG = 4
ATT_UNROLL = 4
ATT_LAG = 2

Q_SCALE = (QK_NOPE_DIM + QK_ROPE_DIM) ** -0.5 * math.log2(math.e)
NEG_BIG = -1e30
SHIFT_BOUND_MAX = 50.0
ROUNDING_SLACK = 1.01


def _resident(shape):
    n = len(shape)
    return pl.BlockSpec(shape, lambda *_: (0,) * n, pipeline_mode=pl.Buffered(1))


def _modulated_norm(x, nw, shift, scale):
    ms = jnp.mean(x * x, axis=-1, keepdims=True)
    return x * lax.rsqrt(ms + EPS) * (nw * (1.0 + scale)) + shift


def _rope_kernel(pos_ref, invf_ref, cos_ref, sin_ref):
    ang = pos_ref[0].astype(jnp.float32) * invf_ref[...]
    cos_ref[0] = jnp.cos(ang)
    sin_ref[0] = jnp.sin(ang)


def _rope_tables(positions):
    b, s = positions.shape
    inv_freq = 1.0 / (ROPE_THETA ** (jnp.arange(0, QK_ROPE_DIM, 2, dtype=jnp.float32) / QK_ROPE_DIM))
    invf = jnp.broadcast_to(inv_freq[:, None], (ROPE_HALF, s))
    out = jax.ShapeDtypeStruct((b, ROPE_HALF, s), jnp.float32)
    return pl.pallas_call(
        _rope_kernel,
        grid=(b,),
        in_specs=[pl.BlockSpec((1, 1, s), lambda i: (i, 0, 0)),
                  pl.BlockSpec((ROPE_HALF, s), lambda i: (0, 0))],
        out_specs=[pl.BlockSpec((1, ROPE_HALF, s), lambda i: (i, 0, 0))] * 2,
        out_shape=[out, out],
        name="rope_tables",
    )(positions.reshape(b, 1, s), invf)


def _mod_kernel(c_ref, w_ref, b_ref, o_ref):
    c = c_ref[...]
    c_act = c * jax.nn.sigmoid(c)
    o_ref[...] = jnp.dot(c_act.astype(jnp.bfloat16), w_ref[0].astype(jnp.bfloat16),
                         preferred_element_type=jnp.float32) + b_ref[0]


def _modulation(c, w_ada, b_ada, layer):
    b, d = c.shape
    depth, _, n = w_ada.shape
    tn = 1024
    return pl.pallas_call(
        _mod_kernel,
        grid=(n // tn,),
        in_specs=[pl.BlockSpec((b, d), lambda j: (0, 0)),
                  pl.BlockSpec((1, d, tn), lambda j: (layer, 0, j)),
                  pl.BlockSpec((1, 1, tn), lambda j: (layer, 0, j))],
        out_specs=pl.BlockSpec((b, tn), lambda j: (0, j)),
        out_shape=jax.ShapeDtypeStruct((b, n), jnp.float32),
        name="adaln_mod",
    )(c, w_ada, b_ada.reshape(depth, 1, n))


def _ffn_rows(xs, sub, mod_ref, nw_ref, w13_ref, w2_ref, o_ref):
    shift = mod_ref[0, 3 * sub + 0:3 * sub + 1, :]
    scale = mod_ref[0, 3 * sub + 1:3 * sub + 2, :]
    gate = 0.5 * mod_ref[0, 3 * sub + 2:3 * sub + 3, :]
    nw = nw_ref[sub:sub + 1, :]
    rows = xs[0].shape[0]
    hs = [_modulated_norm(x, nw, shift, scale).astype(jnp.bfloat16) for x in xs]
    gus = [jnp.dot(h, w13_ref[...], preferred_element_type=jnp.float32) for h in hs]
    for r, x in enumerate(xs):
        g = gus[r][:, :D_FF]
        u = gus[r][:, D_FF:]
        a = (g * jax.nn.sigmoid(g) * u).astype(jnp.bfloat16)
        y = jnp.dot(a, w2_ref[...], preferred_element_type=jnp.float32)
        o_ref[0, r * rows:(r + 1) * rows, :] = x + gate * y


def _ffn_kernel(sub, x_ref, mod_ref, nw_ref, w13_ref, w2_ref, o_ref):
    rows = x_ref.shape[1] // FFN_SPLIT
    xs = [x_ref[0, r * rows:(r + 1) * rows, :] for r in range(FFN_SPLIT)]
    _ffn_rows(xs, sub, mod_ref, nw_ref, w13_ref, w2_ref, o_ref)


def _ffn(x, mod, norm_w, w13, w2, sub):
    b, s, d = x.shape
    tm = FFN_TM
    return pl.pallas_call(
        functools.partial(_ffn_kernel, sub),
        grid=(b, s // tm),
        in_specs=[pl.BlockSpec((1, tm, d), lambda i, j: (i, j, 0)),
                  pl.BlockSpec((1, 3 * N_SUB, d), lambda i, j: (i, 0, 0)),
                  _resident(norm_w.shape),
                  _resident(w13.shape),
                  _resident(w2.shape)],
        out_specs=pl.BlockSpec((1, tm, d), lambda i, j: (i, j, 0)),
        out_shape=jax.ShapeDtypeStruct(x.shape, jnp.float32),
        compiler_params=pltpu.CompilerParams(
            dimension_semantics=("arbitrary", "arbitrary"), vmem_limit_bytes=VMEM_LIMIT),
        name=f"ffn{sub}",
    )(x, mod, norm_w, w13, w2)


def _rms_rows(x, w):
    ms = jnp.mean(x * x, axis=0, keepdims=True)
    return x * lax.rsqrt(ms + EPS) * w


def _rope_rows(y, cos, sin):
    y1 = y[:ROPE_HALF]
    y2 = y[ROPE_HALF:]
    return y1 * cos - y2 * sin, y2 * cos + y1 * sin


def _mix_kernel(shift_ref, x_ref, mod_ref, nw_ref, cos_ref, sin_ref, wa_ref, wcg_ref, wuq_ref, wukv_ref,
                colw_ref, convw_ref, wb1_ref,
                q_ref, k_ref, v_ref, g0_ref, mc_ref, tail_ref):
    tm = x_ref.shape[1]
    rep = tm // LANES
    j = pl.program_id(1)

    @pl.when(j == 0)
    def _():
        tail_ref[...] = jnp.zeros_like(tail_ref)

    half = tm // 2
    hs = [_modulated_norm(x_ref[0, r * half:(r + 1) * half, :], nw_ref[1:2, :], mod_ref[0, 3:4, :],
                          mod_ref[0, 4:5, :]).astype(jnp.bfloat16) for r in range(2)]

    def rows_dot(w):
        return jnp.concatenate([jnp.dot(h, w, preferred_element_type=jnp.float32) for h in hs], axis=0)

    zt = jnp.concatenate([lax.dot_general(wa_ref[...], h, (((1,), (1,)), ((), ())),
                                          preferred_element_type=jnp.float32) for h in hs],
                         axis=1)
    cv = rows_dot(wcg_ref[:, 0:3 * CONV_WIDTH])

    def colw(lo, n):
        return jnp.tile(colw_ref[lo:lo + n, :], (1, rep))

    o_qa, o_kva = 0, Q_LORA_RANK
    o_qn = o_kva + KV_LORA_RANK
    o_kn = o_qn + QK_NOPE_DIM
    o_qr = o_kn + QK_NOPE_DIM
    o_kr = o_qr + QK_ROPE_DIM

    cq = _rms_rows(zt[0:Q_LORA_RANK], colw(o_qa, Q_LORA_RANK)).astype(jnp.bfloat16)
    ckv = _rms_rows(zt[Q_LORA_RANK:Q_LORA_RANK + KV_LORA_RANK], colw(o_kva, KV_LORA_RANK)).astype(jnp.bfloat16)
    kr_raw = zt[Q_LORA_RANK + KV_LORA_RANK:A_ROWS]
    cos = cos_ref[0]
    sin = sin_ref[0]
    kr1, kr2 = _rope_rows(_rms_rows(kr_raw, colw(o_kr, QK_ROPE_DIM)), cos, sin)

    gl0 = rows_dot(wcg_ref[:, 3 * CONV_WIDTH:3 * CONV_WIDTH + D_MODEL])
    qt = jnp.dot(wuq_ref[...], cq, preferred_element_type=jnp.float32)
    kvt = jnp.dot(wukv_ref[...], ckv, preferred_element_type=jnp.float32)
    gl1 = rows_dot(wcg_ref[:, 3 * CONV_WIDTH + D_MODEL:CG_COLS])

    xv = cv[:, 0:CONV_WIDTH]
    gate_b = cv[:, CONV_WIDTH:2 * CONV_WIDTH]
    gate_c = cv[:, 2 * CONV_WIDTH:3 * CONV_WIDTH]
    u = gate_c * xv

    p1 = tail_ref[7:8, :]
    p2 = tail_ref[6:7, :]
    row = lax.broadcasted_iota(jnp.int32, u.shape, 0)
    r1 = jnp.where(row == 0, p1, pltpu.roll(u, 1, 0))
    r2 = jnp.where(row == 0, p2, jnp.where(row == 1, p1, pltpu.roll(u, 2, 0)))
    tail_ref[...] = u[tm - 8:tm, :]
    conv = convw_ref[0:1, :] * r2 + convw_ref[1:2, :] * r1 + convw_ref[2:3, :] * u
    o_conv = (gate_b * conv).astype(jnp.bfloat16)
    y_conv = jnp.dot(o_conv, wb1_ref[...], preferred_element_type=jnp.float32)
    g0_ref[0] = jax.nn.sigmoid(gl0).astype(jnp.bfloat16)
    mc_ref[0] = (jax.nn.sigmoid(gl1) * y_conv).astype(jnp.bfloat16)

    wqn = colw(o_qn, QK_NOPE_DIM)
    wkn = colw(o_kn, QK_NOPE_DIM)
    wqr = colw(o_qr, QK_ROPE_DIM)
    shift = shift_ref[0]
    folded = jnp.where(shift <= SHIFT_BOUND_MAX, shift, 0.0)
    pad_row = lax.broadcasted_iota(jnp.int32, (HEAD_PAD - QK_NOPE_DIM - QK_ROPE_DIM, tm), 0)
    k_pad = jnp.where(pad_row == 0, 1.0, 0.0)
    q_pad = jnp.where(pad_row == 0, -folded, 0.0).astype(jnp.bfloat16)
    ones = jnp.ones((V_ROWS - V_HEAD_DIM, tm), jnp.bfloat16)
    for hd in range(N_HEADS):
        base = hd * HEAD_PAD
        qn = _rms_rows(qt[base:base + QK_NOPE_DIM], wqn)
        qr1, qr2 = _rope_rows(
            _rms_rows(qt[base + QK_NOPE_DIM:base + QK_NOPE_DIM + QK_ROPE_DIM], wqr), cos, sin)
        q_ref[0, hd, 0:QK_NOPE_DIM, :] = qn.astype(jnp.bfloat16)
        q_ref[0, hd, QK_NOPE_DIM:QK_NOPE_DIM + ROPE_HALF, :] = qr1.astype(jnp.bfloat16)
        q_ref[0, hd, QK_NOPE_DIM + ROPE_HALF:QK_NOPE_DIM + QK_ROPE_DIM, :] = qr2.astype(jnp.bfloat16)
        q_ref[0, hd, QK_NOPE_DIM + QK_ROPE_DIM:HEAD_PAD, :] = q_pad

        kn = _rms_rows(kvt[base:base + QK_NOPE_DIM], wkn)
        kt = jnp.concatenate([kn, kr1, kr2, k_pad], axis=0)
        k_ref[0, hd] = kt.T.astype(jnp.bfloat16)

        v_ref[0, hd, 0:V_HEAD_DIM, :] = kvt[base + QK_NOPE_DIM:base + HEAD_PAD].astype(jnp.bfloat16)
        v_ref[0, hd, V_HEAD_DIM:V_ROWS, :] = ones


def _mixer_inputs(shift, x, mod, norm_w, cos_t, sin_t, wa_t, wcg, wuq_t, wukv_t, colw, conv_w, wb1):
    b, s, d = x.shape
    tm = MIX_TM
    bf = jnp.bfloat16
    out_shape = [
        jax.ShapeDtypeStruct((b, N_HEADS, HEAD_PAD, s), bf),
        jax.ShapeDtypeStruct((b, N_HEADS, s, HEAD_PAD), bf),
        jax.ShapeDtypeStruct((b, N_HEADS, V_ROWS, s), bf),
        jax.ShapeDtypeStruct((b, s, d), bf),
        jax.ShapeDtypeStruct((b, s, d), bf),
    ]
    out_specs = [
        pl.BlockSpec((1, N_HEADS, HEAD_PAD, tm), lambda i, j: (i, 0, 0, j)),
        pl.BlockSpec((1, N_HEADS, tm, HEAD_PAD), lambda i, j: (i, 0, j, 0)),
        pl.BlockSpec((1, N_HEADS, V_ROWS, tm), lambda i, j: (i, 0, 0, j)),
        pl.BlockSpec((1, tm, d), lambda i, j: (i, j, 0)),
        pl.BlockSpec((1, tm, d), lambda i, j: (i, j, 0)),
    ]
    in_specs = [
        pl.BlockSpec(memory_space=pltpu.SMEM),
        pl.BlockSpec((1, tm, d), lambda i, j: (i, j, 0)),
        pl.BlockSpec((1, 3 * N_SUB, d), lambda i, j: (i, 0, 0)),
        _resident(norm_w.shape),
        pl.BlockSpec((1, ROPE_HALF, tm), lambda i, j: (i, 0, j)),
        pl.BlockSpec((1, ROPE_HALF, tm), lambda i, j: (i, 0, j)),
        _resident(wa_t.shape), _resident(wcg.shape), _resident(wuq_t.shape), _resident(wukv_t.shape),
        _resident(colw.shape), _resident(conv_w.shape), _resident(wb1.shape),
    ]
    return pl.pallas_call(
        _mix_kernel,
        grid=(b, s // tm),
        in_specs=in_specs,
        out_specs=out_specs,
        out_shape=out_shape,
        scratch_shapes=[pltpu.VMEM((8, CONV_WIDTH), jnp.float32)],
        compiler_params=pltpu.CompilerParams(
            dimension_semantics=("arbitrary", "arbitrary"), vmem_limit_bytes=VMEM_LIMIT),
        name="mixer_inputs",
    )(shift, x, mod, norm_w, cos_t, sin_t, wa_t, wcg, wuq_t, wukv_t, colw, conv_w, wb1)


def _attn_kernel(shift_ref, q_ref, k_ref, v_ref, o_ref, m_ref, acc_ref):
    tq = q_ref.shape[3]
    tk = ATT_TK
    i = pl.program_id(2)
    diag = pl.multiple_of(i * tq, tq)
    n_chunks = i * (tq // tk)
    shift = shift_ref[0]

    key = lax.broadcasted_iota(jnp.int32, (tq, tq), 0)
    qry = lax.broadcasted_iota(jnp.int32, (tq, tq), 1)
    visible = key <= qry

    def shifted_chunks(chunks):
        items = [(c, hd) for c in range(len(chunks)) for hd in range(ATT_HG)]
        pending = {}
        out = [None] * ATT_HG
        for n in range(len(items) + ATT_LAG):
            if n < len(items):
                c, hd = items[n]
                start, size, masked = chunks[c]
                s = jnp.dot(k_ref[0, hd, pl.ds(start, size), :], q_ref[0, hd],
                            preferred_element_type=jnp.float32)
                pending[n] = jnp.where(visible, s, NEG_BIG) if masked else s
            if n >= ATT_LAG:
                c, hd = items[n - ATT_LAG]
                start, size, _ = chunks[c]
                p = jnp.exp2(pending.pop(n - ATT_LAG)).astype(jnp.bfloat16)
                pv = jnp.dot(v_ref[0, hd, :, pl.ds(start, size)], p, preferred_element_type=jnp.float32)
                out[hd] = pv if out[hd] is None else out[hd] + pv
        return out

    @pl.when(shift <= SHIFT_BOUND_MAX)
    def _():
        n_groups = n_chunks // ATT_UNROLL
        n_rest = n_chunks - n_groups * ATT_UNROLL

        for rest in range(ATT_UNROLL):
            @pl.when(n_rest == rest)
            def _():
                first = n_groups * (ATT_UNROLL * tk)
                chunks = [(pl.multiple_of(first + u * tk, tk), tk, False) for u in range(rest)]
                for hd, pv in enumerate(shifted_chunks(chunks + [(diag, tq, True)])):
                    acc_ref[hd] = pv

        def group_body(g, carry):
            base = g * (ATT_UNROLL * tk)
            chunks = [(pl.multiple_of(base + u * tk, tk), tk, False) for u in range(ATT_UNROLL)]
            for hd, pv in enumerate(shifted_chunks(chunks)):
                acc_ref[hd] += pv
            return carry

        lax.fori_loop(0, n_groups, group_body, 0)

    @pl.when(shift > SHIFT_BOUND_MAX)
    def _():
        def scores(start, n):
            return [jnp.dot(k_ref[0, hd, pl.ds(start, n), :], q_ref[0, hd],
                            preferred_element_type=jnp.float32) for hd in range(ATT_HG)]

        for hd, s in enumerate(scores(diag, tq)):
            s = jnp.where(visible, s, NEG_BIG)
            m = jnp.max(s, axis=0, keepdims=True)
            p = jnp.exp2(s - m).astype(jnp.bfloat16)
            acc_ref[hd] = jnp.dot(v_ref[0, hd, :, pl.ds(diag, tq)], p, preferred_element_type=jnp.float32)
            m_ref[hd] = m

        def body(j, carry):
            start = pl.multiple_of(j * tk, tk)
            for hd, s in enumerate(scores(start, tk)):
                m_old = m_ref[hd]
                m_new = jnp.maximum(m_old, jnp.max(s, axis=0, keepdims=True))
                p = jnp.exp2(s - m_new).astype(jnp.bfloat16)
                alpha = jnp.exp2(m_old - m_new)
                pv = jnp.dot(v_ref[0, hd, :, pl.ds(start, tk)], p, preferred_element_type=jnp.float32)
                acc_ref[hd] = alpha * acc_ref[hd] + pv
                m_ref[hd] = m_new
            return carry

        lax.fori_loop(0, n_chunks, body, 0)

    for hd in range(ATT_HG):
        acc = acc_ref[hd]
        o_ref[0, hd] = (acc[0:V_HEAD_DIM] / acc[V_HEAD_DIM:V_HEAD_DIM + 1]).astype(o_ref.dtype)


def _attention(shift, q_t, k, v_t):
    b, nh, _, s = q_t.shape
    tq = ATT_TQ
    hg = ATT_HG
    return pl.pallas_call(
        _attn_kernel,
        grid=(b, nh // hg, s // tq),
        in_specs=[pl.BlockSpec(memory_space=pltpu.SMEM),
                  pl.BlockSpec((1, hg, HEAD_PAD, tq), lambda bi, g, i: (bi, g, 0, i)),
                  pl.BlockSpec((1, hg, s, HEAD_PAD), lambda bi, g, i: (bi, g, 0, 0)),
                  pl.BlockSpec((1, hg, V_ROWS, s), lambda bi, g, i: (bi, g, 0, 0))],
        out_specs=pl.BlockSpec((1, hg, V_HEAD_DIM, tq), lambda bi, g, i: (bi, g, 0, i)),
        out_shape=jax.ShapeDtypeStruct((b, nh, V_HEAD_DIM, s), jnp.bfloat16),
        scratch_shapes=[pltpu.VMEM((hg, 1, tq), jnp.float32),
                        pltpu.VMEM((hg, V_ROWS, tq), jnp.float32)],
        compiler_params=pltpu.CompilerParams(
            dimension_semantics=("arbitrary", "arbitrary", "arbitrary"), vmem_limit_bytes=VMEM_LIMIT),
        name="mla_attention",
    )(shift, q_t, k, v_t)


def _post_kernel(x_ref, mod_ref, nw_ref, o_ref, g0_ref, mc_ref, wb0_ref, wout_ref, w13_ref, w2_ref, y_ref):
    rows = x_ref.shape[1] // FFN_SPLIT
    gate = mod_ref[0, 5:6, :]
    xs = []
    for r in range(FFN_SPLIT):
        sl = slice(r * rows, (r + 1) * rows)
        y_attn = lax.dot_general(o_ref[0, :, sl], wb0_ref[...], (((0,), (0,)), ((), ())),
                                 preferred_element_type=jnp.float32)
        merged = g0_ref[0, sl, :].astype(jnp.float32) * y_attn + mc_ref[0, sl, :].astype(jnp.float32)
        y = jnp.dot(merged.astype(jnp.bfloat16), wout_ref[...], preferred_element_type=jnp.float32)
        xs.append(x_ref[0, sl, :] + gate * y)
    _ffn_rows(xs, 2, mod_ref, nw_ref, w13_ref, w2_ref, y_ref)


def _post(x, mod, norm_w, o_t, g0, mc, wb0, wout, w13, w2):
    b, s, d = x.shape
    tm = POST_TM
    rows = o_t.shape[1]
    return pl.pallas_call(
        _post_kernel,
        grid=(b, s // tm),
        in_specs=[pl.BlockSpec((1, tm, d), lambda i, j: (i, j, 0)),
                  pl.BlockSpec((1, 3 * N_SUB, d), lambda i, j: (i, 0, 0)),
                  _resident(norm_w.shape),
                  pl.BlockSpec((1, rows, tm), lambda i, j: (i, 0, j)),
                  pl.BlockSpec((1, tm, d), lambda i, j: (i, j, 0)),
                  pl.BlockSpec((1, tm, d), lambda i, j: (i, j, 0)),
                  _resident(wb0.shape), _resident(wout.shape), _resident(w13.shape), _resident(w2.shape)],
        out_specs=pl.BlockSpec((1, tm, d), lambda i, j: (i, j, 0)),
        out_shape=jax.ShapeDtypeStruct(x.shape, jnp.float32),
        compiler_params=pltpu.CompilerParams(
            dimension_semantics=("arbitrary", "arbitrary"), vmem_limit_bytes=VMEM_LIMIT),
        name="merge_out_ffn2",
    )(x, mod, norm_w, o_t, g0, mc, wb0, wout, w13, w2)


def _score_bound(wqn, wqr, wkn, wkr):
    def sq(w_nope, w_rope):
        return QK_NOPE_DIM * jnp.max(jnp.square(w_nope)) + QK_ROPE_DIM * jnp.max(jnp.square(w_rope))
    bound = Q_SCALE * jnp.sqrt(sq(wqn, wqr) * sq(wkn, wkr)) * ROUNDING_SLACK
    return bound.reshape(1).astype(jnp.float32)


def _bcast_cols(v):
    return jnp.broadcast_to(v[:, None], (v.shape[0], LANES))


def kernel(x, c, positions, w_ada, b_ada, norm_w, ffn_w13, ffn_w2, w_in, q_a_norm, w_uq, kv_a_norm,
           w_ukv, q_norm_nope, k_norm_nope, q_norm_rope, k_norm_rope, conv_w, w_branch, w_out):
    bf = jnp.bfloat16
    b, s, d = x.shape
    depth = w_ada.shape[0]
    cos_t, sin_t = _rope_tables(positions)
    for l in range(depth):
        mod = _modulation(c, w_ada, b_ada, l).reshape(b, 3 * N_SUB, d)

        wi = w_in[l]
        wa_t = wi[:, :A_ROWS].T.astype(bf)
        wcg = wi[:, A_ROWS:].astype(bf)
        wuq = w_uq[l].reshape(Q_LORA_RANK, N_HEADS, QK_NOPE_DIM + QK_ROPE_DIM)
        wuq = jnp.pad(wuq, ((0, 0), (0, 0), (0, HEAD_PAD - QK_NOPE_DIM - QK_ROPE_DIM)))
        wuq_t = wuq.reshape(Q_LORA_RANK, N_HEADS * HEAD_PAD).T.astype(bf)
        wukv_t = w_ukv[l].T.astype(bf)
        colw = jnp.concatenate([_bcast_cols(q_a_norm[l]), _bcast_cols(kv_a_norm[l]),
                                _bcast_cols(q_norm_nope[l] * Q_SCALE), _bcast_cols(k_norm_nope[l]),
                                _bcast_cols(q_norm_rope[l] * Q_SCALE), _bcast_cols(k_norm_rope[l])], axis=0)
        shift = _score_bound(q_norm_nope[l], q_norm_rope[l], k_norm_nope[l], k_norm_rope[l])

        x = _ffn(x, mod, norm_w[l], ffn_w13[l, 0].astype(bf), ffn_w2[l, 0].astype(bf), 0)
        q_t, k, v_t, g0, mc = _mixer_inputs(shift, x, mod, norm_w[l], cos_t, sin_t, wa_t, wcg, wuq_t, wukv_t,
                                            colw, conv_w[l], w_branch[l, 1].astype(bf))
        o_t = _attention(shift, q_t, k, v_t)
        x = _post(x, mod, norm_w[l], o_t.reshape(b, N_HEADS * V_HEAD_DIM, s), g0, mc,
                  w_branch[l, 0].astype(bf), w_out[l].astype(bf),
                  ffn_w13[l, 1].astype(bf), ffn_w2[l, 1].astype(bf))
    return x
```

```python
import functools
import math

import jax
import jax.numpy as jnp
from jax import lax
from jax.experimental import pallas as pl
from jax.experimental.pallas import tpu as pltpu

D_MODEL = 1024
N_HEADS = 8
QK_NOPE_DIM = 64
QK_ROPE_DIM = 32
V_HEAD_DIM = 64
Q_LORA_RANK = 256
KV_LORA_RANK = 128
CONV_WIDTH = 512
CONV_K = 3
D_FF = 2816
N_SUB = 3
ROPE_THETA = 10000.0
EPS = 1e-6

LANES = 128
HEAD_PAD = 128
ROPE_HALF = QK_ROPE_DIM // 2
V_ROWS = V_HEAD_DIM + 16
A_ROWS = Q_LORA_RANK + KV_LORA_RANK + QK_ROPE_DIM
CG_COLS = 3 * CONV_WIDTH + 2 * D_MODEL
VMEM_LIMIT = 56 * 1024 * 1024

FFN_TM = 512
FFN_SPLIT = 2
MIX_TM = 512
POST_TM = 512
ATT_TQ = 512
ATT_TK = 256
ATT_HG = 4
ATT_UNROLL = 8
ATT_LAG = 4

Q_SCALE = (QK_NOPE_DIM + QK_ROPE_DIM) ** -0.5 * math.log2(math.e)
NEG_BIG = -1e30
SHIFT_BOUND_MAX = 50.0
ROUNDING_SLACK = 1.01


def _resident(shape):
    n = len(shape)
    return pl.BlockSpec(shape, lambda *_: (0,) * n, pipeline_mode=pl.Buffered(1))


def _modulated_norm(x, nw, shift, scale):
    ms = jnp.mean(x * x, axis=-1, keepdims=True)
    return x * lax.rsqrt(ms + EPS) * (nw * (1.0 + scale)) + shift


def _rope_kernel(pos_ref, invf_ref, cos_ref, sin_ref):
    ang = pos_ref[0].astype(jnp.float32) * invf_ref[...]
    cos_ref[0] = jnp.cos(ang)
    sin_ref[0] = jnp.sin(ang)


def _rope_tables(positions):
    b, s = positions.shape
    inv_freq = 1.0 / (ROPE_THETA ** (jnp.arange(0, QK_ROPE_DIM, 2, dtype=jnp.float32) / QK_ROPE_DIM))
    invf = jnp.broadcast_to(inv_freq[:, None], (ROPE_HALF, s))
    out = jax.ShapeDtypeStruct((b, ROPE_HALF, s), jnp.float32)
    return pl.pallas_call(
        _rope_kernel,
        grid=(b,),
        in_specs=[pl.BlockSpec((1, 1, s), lambda i: (i, 0, 0)),
                  pl.BlockSpec((ROPE_HALF, s), lambda i: (0, 0))],
        out_specs=[pl.BlockSpec((1, ROPE_HALF, s), lambda i: (i, 0, 0))] * 2,
        out_shape=[out, out],
        name="rope_tables",
    )(positions.reshape(b, 1, s), invf)


def _mod_kernel(c_ref, w_ref, b_ref, o_ref):
    c = c_ref[...]
    c_act = c * jax.nn.sigmoid(c)
    o_ref[...] = jnp.dot(c_act.astype(jnp.bfloat16), w_ref[0].astype(jnp.bfloat16),
                         preferred_element_type=jnp.float32) + b_ref[0]


def _modulation(c, w_ada, b_ada, layer):
    b, d = c.shape
    depth, _, n = w_ada.shape
    tn = 1024
    return pl.pallas_call(
        _mod_kernel,
        grid=(n // tn,),
        in_specs=[pl.BlockSpec((b, d), lambda j: (0, 0)),
                  pl.BlockSpec((1, d, tn), lambda j: (layer, 0, j)),
                  pl.BlockSpec((1, 1, tn), lambda j: (layer, 0, j))],
        out_specs=pl.BlockSpec((b, tn), lambda j: (0, j)),
        out_shape=jax.ShapeDtypeStruct((b, n), jnp.float32),
        name="adaln_mod",
    )(c, w_ada, b_ada.reshape(depth, 1, n))


def _ffn_rows(xs, sub, mod_ref, nw_ref, w13_ref, w2_ref, o_ref):
    shift = mod_ref[0, 3 * sub + 0:3 * sub + 1, :]
    scale = mod_ref[0, 3 * sub + 1:3 * sub + 2, :]
    gate = 0.5 * mod_ref[0, 3 * sub + 2:3 * sub + 3, :]
    nw = nw_ref[sub:sub + 1, :]
    rows = xs[0].shape[0]
    hs = [_modulated_norm(x, nw, shift, scale).astype(jnp.bfloat16) for x in xs]
    gus = [jnp.dot(h, w13_ref[...], preferred_element_type=jnp.float32) for h in hs]
    for r, x in enumerate(xs):
        g = gus[r][:, :D_FF]
        u = gus[r][:, D_FF:]
        a = (g * jax.nn.sigmoid(g) * u).astype(jnp.bfloat16)
        y = jnp.dot(a, w2_ref[...], preferred_element_type=jnp.float32)
        o_ref[0, r * rows:(r + 1) * rows, :] = x + gate * y


def _ffn_kernel(sub, x_ref, mod_ref, nw_ref, w13_ref, w2_ref, o_ref):
    rows = x_ref.shape[1] // FFN_SPLIT
    xs = [x_ref[0, r * rows:(r + 1) * rows, :] for r in range(FFN_SPLIT)]
    _ffn_rows(xs, sub, mod_ref, nw_ref, w13_ref, w2_ref, o_ref)


def _ffn(x, mod, norm_w, w13, w2, sub):
    b, s, d = x.shape
    tm = FFN_TM
    return pl.pallas_call(
        functools.partial(_ffn_kernel, sub),
        grid=(b, s // tm),
        in_specs=[pl.BlockSpec((1, tm, d), lambda i, j: (i, j, 0)),
                  pl.BlockSpec((1, 3 * N_SUB, d), lambda i, j: (i, 0, 0)),
                  _resident(norm_w.shape),
                  _resident(w13.shape),
                  _resident(w2.shape)],
        out_specs=pl.BlockSpec((1, tm, d), lambda i, j: (i, j, 0)),
        out_shape=jax.ShapeDtypeStruct(x.shape, jnp.float32),
        compiler_params=pltpu.CompilerParams(
            dimension_semantics=("arbitrary", "arbitrary"), vmem_limit_bytes=VMEM_LIMIT),
        name=f"ffn{sub}",
    )(x, mod, norm_w, w13, w2)


def _rms_rows(x, w):
    ms = jnp.mean(x * x, axis=0, keepdims=True)
    return x * lax.rsqrt(ms + EPS) * w


def _rope_rows(y, cos, sin):
    y1 = y[:ROPE_HALF]
    y2 = y[ROPE_HALF:]
    return y1 * cos - y2 * sin, y2 * cos + y1 * sin


def _mix_kernel(shift_ref, x_ref, mod_ref, nw_ref, cos_ref, sin_ref, wa_ref, wcg_ref, wuq_ref, wukv_ref,
                colw_ref, convw_ref, wb1_ref,
                q_ref, k_ref, v_ref, g0_ref, mc_ref, tail_ref):
    tm = x_ref.shape[1]
    rep = tm // LANES
    j = pl.program_id(1)

    @pl.when(j == 0)
    def _():
        tail_ref[...] = jnp.zeros_like(tail_ref)

    half = tm // 2
    hs = [_modulated_norm(x_ref[0, r * half:(r + 1) * half, :], nw_ref[1:2, :], mod_ref[0, 3:4, :],
                          mod_ref[0, 4:5, :]).astype(jnp.bfloat16) for r in range(2)]

    def rows_dot(w):
        return jnp.concatenate([jnp.dot(h, w, preferred_element_type=jnp.float32) for h in hs], axis=0)

    zt = jnp.concatenate([lax.dot_general(wa_ref[...], h, (((1,), (1,)), ((), ())),
                                          preferred_element_type=jnp.float32) for h in hs],
                         axis=1)
    cv = rows_dot(wcg_ref[:, 0:3 * CONV_WIDTH])

    def colw(lo, n):
        return jnp.tile(colw_ref[lo:lo + n, :], (1, rep))

    o_qa, o_kva = 0, Q_LORA_RANK
    o_qn = o_kva + KV_LORA_RANK
    o_kn = o_qn + QK_NOPE_DIM
    o_qr = o_kn + QK_NOPE_DIM
    o_kr = o_qr + QK_ROPE_DIM

    cq = _rms_rows(zt[0:Q_LORA_RANK], colw(o_qa, Q_LORA_RANK)).astype(jnp.bfloat16)
    ckv = _rms_rows(zt[Q_LORA_RANK:Q_LORA_RANK + KV_LORA_RANK], colw(o_kva, KV_LORA_RANK)).astype(jnp.bfloat16)
    kr_raw = zt[Q_LORA_RANK + KV_LORA_RANK:A_ROWS]
    cos = cos_ref[0]
    sin = sin_ref[0]
    kr1, kr2 = _rope_rows(_rms_rows(kr_raw, colw(o_kr, QK_ROPE_DIM)), cos, sin)

    gl0 = rows_dot(wcg_ref[:, 3 * CONV_WIDTH:3 * CONV_WIDTH + D_MODEL])
    qt = jnp.dot(wuq_ref[...], cq, preferred_element_type=jnp.float32)
    kvt = jnp.dot(wukv_ref[...], ckv, preferred_element_type=jnp.float32)
    gl1 = rows_dot(wcg_ref[:, 3 * CONV_WIDTH + D_MODEL:CG_COLS])

    xv = cv[:, 0:CONV_WIDTH]
    gate_b = cv[:, CONV_WIDTH:2 * CONV_WIDTH]
    gate_c = cv[:, 2 * CONV_WIDTH:3 * CONV_WIDTH]
    u = gate_c * xv

    p1 = tail_ref[7:8, :]
    p2 = tail_ref[6:7, :]
    row = lax.broadcasted_iota(jnp.int32, u.shape, 0)
    r1 = jnp.where(row == 0, p1, pltpu.roll(u, 1, 0))
    r2 = jnp.where(row == 0, p2, jnp.where(row == 1, p1, pltpu.roll(u, 2, 0)))
    tail_ref[...] = u[tm - 8:tm, :]
    conv = convw_ref[0:1, :] * r2 + convw_ref[1:2, :] * r1 + convw_ref[2:3, :] * u
    o_conv = (gate_b * conv).astype(jnp.bfloat16)
    y_conv = jnp.dot(o_conv, wb1_ref[...], preferred_element_type=jnp.float32)
    g0_ref[0] = jax.nn.sigmoid(gl0).astype(jnp.bfloat16)
    mc_ref[0] = (jax.nn.sigmoid(gl1) * y_conv).astype(jnp.bfloat16)

    wqn = colw(o_qn, QK_NOPE_DIM)
    wkn = colw(o_kn, QK_NOPE_DIM)
    wqr = colw(o_qr, QK_ROPE_DIM)
    shift = shift_ref[0]
    folded = jnp.where(shift <= SHIFT_BOUND_MAX, shift, 0.0)
    pad_row = lax.broadcasted_iota(jnp.int32, (HEAD_PAD - QK_NOPE_DIM - QK_ROPE_DIM, tm), 0)
    k_pad = jnp.where(pad_row == 0, 1.0, 0.0)
    q_pad = jnp.where(pad_row == 0, -folded, 0.0).astype(jnp.bfloat16)
    ones = jnp.ones((V_ROWS - V_HEAD_DIM, tm), jnp.bfloat16)
    for hd in range(N_HEADS):
        base = hd * HEAD_PAD
        qn = _rms_rows(qt[base:base + QK_NOPE_DIM], wqn)
        qr1, qr2 = _rope_rows(
            _rms_rows(qt[base + QK_NOPE_DIM:base + QK_NOPE_DIM + QK_ROPE_DIM], wqr), cos, sin)
        q_ref[0, hd, 0:QK_NOPE_DIM, :] = qn.astype(jnp.bfloat16)
        q_ref[0, hd, QK_NOPE_DIM:QK_NOPE_DIM + ROPE_HALF, :] = qr1.astype(jnp.bfloat16)
        q_ref[0, hd, QK_NOPE_DIM + ROPE_HALF:QK_NOPE_DIM + QK_ROPE_DIM, :] = qr2.astype(jnp.bfloat16)
        q_ref[0, hd, QK_NOPE_DIM + QK_ROPE_DIM:HEAD_PAD, :] = q_pad

        kn = _rms_rows(kvt[base:base + QK_NOPE_DIM], wkn)
        kt = jnp.concatenate([kn, kr1, kr2, k_pad], axis=0)
        k_ref[0, hd] = kt.T.astype(jnp.bfloat16)

        v_ref[0, hd, 0:V_HEAD_DIM, :] = kvt[base + QK_NOPE_DIM:base + HEAD_PAD].astype(jnp.bfloat16)
        v_ref[0, hd, V_HEAD_DIM:V_ROWS, :] = ones


def _mixer_inputs(shift, x, mod, norm_w, cos_t, sin_t, wa_t, wcg, wuq_t, wukv_t, colw, conv_w, wb1):
    b, s, d = x.shape
    tm = MIX_TM
    bf = jnp.bfloat16
    out_shape = [
        jax.ShapeDtypeStruct((b, N_HEADS, HEAD_PAD, s), bf),
        jax.ShapeDtypeStruct((b, N_HEADS, s, HEAD_PAD), bf),
        jax.ShapeDtypeStruct((b, N_HEADS, V_ROWS, s), bf),
        jax.ShapeDtypeStruct((b, s, d), bf),
        jax.ShapeDtypeStruct((b, s, d), bf),
    ]
    out_specs = [
        pl.BlockSpec((1, N_HEADS, HEAD_PAD, tm), lambda i, j: (i, 0, 0, j)),
        pl.BlockSpec((1, N_HEADS, tm, HEAD_PAD), lambda i, j: (i, 0, j, 0)),
        pl.BlockSpec((1, N_HEADS, V_ROWS, tm), lambda i, j: (i, 0, 0, j)),
        pl.BlockSpec((1, tm, d), lambda i, j: (i, j, 0)),
        pl.BlockSpec((1, tm, d), lambda i, j: (i, j, 0)),
    ]
    in_specs = [
        pl.BlockSpec(memory_space=pltpu.SMEM),
        pl.BlockSpec((1, tm, d), lambda i, j: (i, j, 0)),
        pl.BlockSpec((1, 3 * N_SUB, d), lambda i, j: (i, 0, 0)),
        _resident(norm_w.shape),
        pl.BlockSpec((1, ROPE_HALF, tm), lambda i, j: (i, 0, j)),
        pl.BlockSpec((1, ROPE_HALF, tm), lambda i, j: (i, 0, j)),
        _resident(wa_t.shape), _resident(wcg.shape), _resident(wuq_t.shape), _resident(wukv_t.shape),
        _resident(colw.shape), _resident(conv_w.shape), _resident(wb1.shape),
    ]
    return pl.pallas_call(
        _mix_kernel,
        grid=(b, s // tm),
        in_specs=in_specs,
        out_specs=out_specs,
        out_shape=out_shape,
        scratch_shapes=[pltpu.VMEM((8, CONV_WIDTH), jnp.float32)],
        compiler_params=pltpu.CompilerParams(
            dimension_semantics=("arbitrary", "arbitrary"), vmem_limit_bytes=VMEM_LIMIT),
        name="mixer_inputs",
    )(shift, x, mod, norm_w, cos_t, sin_t, wa_t, wcg, wuq_t, wukv_t, colw, conv_w, wb1)


def _attn_kernel(shift_ref, q_ref, k_ref, v_ref, o_ref, m_ref, acc_ref):
    tq = q_ref.shape[3]
    tk = ATT_TK
    i = pl.program_id(2)
    diag = pl.multiple_of(i * tq, tq)
    n_chunks = i * (tq // tk)
    shift = shift_ref[0]

    def causal_mask(rows, key0, q0):
        key = lax.broadcasted_iota(jnp.int32, (rows, tq - q0), 0) + key0
        qry = lax.broadcasted_iota(jnp.int32, (rows, tq - q0), 1) + q0
        return key <= qry

    def shifted_chunks(chunks):
        items = [(c, hd) for c in range(len(chunks)) for hd in range(ATT_HG)]
        n_slabs = tq // tk
        pending = {}
        out = [[None] * n_slabs for _ in range(ATT_HG)]
        for n in range(len(items) + ATT_LAG):
            if n < len(items):
                c, hd = items[n]
                start, mask, q0 = chunks[c]
                s = jnp.dot(k_ref[0, hd, pl.ds(start, tk), :], q_ref[0, hd, :, q0:tq],
                            preferred_element_type=jnp.float32)
                pending[n] = s if mask is None else jnp.where(mask, s, NEG_BIG)
            if n >= ATT_LAG:
                c, hd = items[n - ATT_LAG]
                start, _, q0 = chunks[c]
                p = jnp.exp2(pending.pop(n - ATT_LAG)).astype(jnp.bfloat16)
                pv = jnp.dot(v_ref[0, hd, :, pl.ds(start, tk)], p, preferred_element_type=jnp.float32)
                for slab in range(q0 // tk, n_slabs):
                    part = pv[:, slab * tk - q0:(slab + 1) * tk - q0]
                    out[hd][slab] = part if out[hd][slab] is None else out[hd][slab] + part
        return out

    diag_chunks = [(pl.multiple_of(diag + d * tk, tk), causal_mask(tk, d * tk, d * tk), d * tk)
                   for d in range(tq // tk)]

    @pl.when(shift <= SHIFT_BOUND_MAX)
    def _():
        n_groups = n_chunks // ATT_UNROLL
        n_rest = n_chunks - n_groups * ATT_UNROLL

        for rest in range(0, ATT_UNROLL, tq // tk):
            @pl.when(n_rest == rest)
            def _():
                first = n_groups * (ATT_UNROLL * tk)
                chunks = [(pl.multiple_of(first + u * tk, tk), None, 0) for u in range(rest)]
                for hd, slabs in enumerate(shifted_chunks(chunks + diag_chunks)):
                    for slab, pv in enumerate(slabs):
                        acc_ref[hd, :, slab * tk:(slab + 1) * tk] = pv

        def group_body(g, carry):
            base = g * (ATT_UNROLL * tk)
            chunks = [(pl.multiple_of(base + u * tk, tk), None, 0) for u in range(ATT_UNROLL)]
            for hd, slabs in enumerate(shifted_chunks(chunks)):
                for slab, pv in enumerate(slabs):
                    acc_ref[hd, :, slab * tk:(slab + 1) * tk] += pv
            return carry

        lax.fori_loop(0, n_groups, group_body, 0)

    @pl.when(shift > SHIFT_BOUND_MAX)
    def _():
        def scores(start, n):
            return [jnp.dot(k_ref[0, hd, pl.ds(start, n), :], q_ref[0, hd],
                            preferred_element_type=jnp.float32) for hd in range(ATT_HG)]

        visible = causal_mask(tq, 0, 0)
        for hd, s in enumerate(scores(diag, tq)):
            s = jnp.where(visible, s, NEG_BIG)
            m = jnp.max(s, axis=0, keepdims=True)
            p = jnp.exp2(s - m).astype(jnp.bfloat16)
            acc_ref[hd] = jnp.dot(v_ref[0, hd, :, pl.ds(diag, tq)], p, preferred_element_type=jnp.float32)
            m_ref[hd] = m

        def body(j, carry):
            start = pl.multiple_of(j * tk, tk)
            for hd, s in enumerate(scores(start, tk)):
                m_old = m_ref[hd]
                m_new = jnp.maximum(m_old, jnp.max(s, axis=0, keepdims=True))
                p = jnp.exp2(s - m_new).astype(jnp.bfloat16)
                alpha = jnp.exp2(m_old - m_new)
                pv = jnp.dot(v_ref[0, hd, :, pl.ds(start, tk)], p, preferred_element_type=jnp.float32)
                acc_ref[hd] = alpha * acc_ref[hd] + pv
                m_ref[hd] = m_new
            return carry

        lax.fori_loop(0, n_chunks, body, 0)

    for hd in range(ATT_HG):
        acc = acc_ref[hd]
        o_ref[0, hd] = (acc[0:V_HEAD_DIM] / acc[V_HEAD_DIM:V_HEAD_DIM + 1]).astype(o_ref.dtype)


def _attention(shift, q_t, k, v_t):
    b, nh, _, s = q_t.shape
    tq = ATT_TQ
    hg = ATT_HG
    return pl.pallas_call(
        _attn_kernel,
        grid=(b, nh // hg, s // tq),
        in_specs=[pl.BlockSpec(memory_space=pltpu.SMEM),
                  pl.BlockSpec((1, hg, HEAD_PAD, tq), lambda bi, g, i: (bi, g, 0, i)),
                  pl.BlockSpec((1, hg, s, HEAD_PAD), lambda bi, g, i: (bi, g, 0, 0)),
                  pl.BlockSpec((1, hg, V_ROWS, s), lambda bi, g, i: (bi, g, 0, 0))],
        out_specs=pl.BlockSpec((1, hg, V_HEAD_DIM, tq), lambda bi, g, i: (bi, g, 0, i)),
        out_shape=jax.ShapeDtypeStruct((b, nh, V_HEAD_DIM, s), jnp.bfloat16),
        scratch_shapes=[pltpu.VMEM((hg, 1, tq), jnp.float32),
                        pltpu.VMEM((hg, V_ROWS, tq), jnp.float32)],
        compiler_params=pltpu.CompilerParams(
            dimension_semantics=("arbitrary", "arbitrary", "arbitrary"), vmem_limit_bytes=VMEM_LIMIT),
        name="mla_attention",
    )(shift, q_t, k, v_t)


def _post_kernel(x_ref, mod_ref, nw_ref, o_ref, g0_ref, mc_ref, wb0_ref, wout_ref, w13_ref, w2_ref, y_ref):
    rows = x_ref.shape[1] // FFN_SPLIT
    gate = mod_ref[0, 5:6, :]
    xs = []
    for r in range(FFN_SPLIT):
        sl = slice(r * rows, (r + 1) * rows)
        y_attn = lax.dot_general(o_ref[0, :, sl], wb0_ref[...], (((0,), (0,)), ((), ())),
                                 preferred_element_type=jnp.float32)
        merged = g0_ref[0, sl, :].astype(jnp.float32) * y_attn + mc_ref[0, sl, :].astype(jnp.float32)
        y = jnp.dot(merged.astype(jnp.bfloat16), wout_ref[...], preferred_element_type=jnp.float32)
        xs.append(x_ref[0, sl, :] + gate * y)
    _ffn_rows(xs, 2, mod_ref, nw_ref, w13_ref, w2_ref, y_ref)


def _post(x, mod, norm_w, o_t, g0, mc, wb0, wout, w13, w2):
    b, s, d = x.shape
    tm = POST_TM
    rows = o_t.shape[1]
    return pl.pallas_call(
        _post_kernel,
        grid=(b, s // tm),
        in_specs=[pl.BlockSpec((1, tm, d), lambda i, j: (i, j, 0)),
                  pl.BlockSpec((1, 3 * N_SUB, d), lambda i, j: (i, 0, 0)),
                  _resident(norm_w.shape),
                  pl.BlockSpec((1, rows, tm), lambda i, j: (i, 0, j)),
                  pl.BlockSpec((1, tm, d), lambda i, j: (i, j, 0)),
                  pl.BlockSpec((1, tm, d), lambda i, j: (i, j, 0)),
                  _resident(wb0.shape), _resident(wout.shape), _resident(w13.shape), _resident(w2.shape)],
        out_specs=pl.BlockSpec((1, tm, d), lambda i, j: (i, j, 0)),
        out_shape=jax.ShapeDtypeStruct(x.shape, jnp.float32),
        compiler_params=pltpu.CompilerParams(
            dimension_semantics=("arbitrary", "arbitrary"), vmem_limit_bytes=VMEM_LIMIT),
        name="merge_out_ffn2",
    )(x, mod, norm_w, o_t, g0, mc, wb0, wout, w13, w2)


def _score_bound(wqn, wqr, wkn, wkr):
    def sq(w_nope, w_rope):
        return QK_NOPE_DIM * jnp.max(jnp.square(w_nope)) + QK_ROPE_DIM * jnp.max(jnp.square(w_rope))
    bound = Q_SCALE * jnp.sqrt(sq(wqn, wqr) * sq(wkn, wkr)) * ROUNDING_SLACK
    return bound.reshape(1).astype(jnp.float32)


def _bcast_cols(v):
    return jnp.broadcast_to(v[:, None], (v.shape[0], LANES))


def kernel(x, c, positions, w_ada, b_ada, norm_w, ffn_w13, ffn_w2, w_in, q_a_norm, w_uq, kv_a_norm,
           w_ukv, q_norm_nope, k_norm_nope, q_norm_rope, k_norm_rope, conv_w, w_branch, w_out):
    bf = jnp.bfloat16
    b, s, d = x.shape
    depth = w_ada.shape[0]
    cos_t, sin_t = _rope_tables(positions)
    for l in range(depth):
        mod = _modulation(c, w_ada, b_ada, l).reshape(b, 3 * N_SUB, d)

        wi = w_in[l]
        wa_t = wi[:, :A_ROWS].T.astype(bf)
        wcg = wi[:, A_ROWS:].astype(bf)
        wuq = w_uq[l].reshape(Q_LORA_RANK, N_HEADS, QK_NOPE_DIM + QK_ROPE_DIM)
        wuq = jnp.pad(wuq, ((0, 0), (0, 0), (0, HEAD_PAD - QK_NOPE_DIM - QK_ROPE_DIM)))
        wuq_t = wuq.reshape(Q_LORA_RANK, N_HEADS * HEAD_PAD).T.astype(bf)
        wukv_t = w_ukv[l].T.astype(bf)
        colw = jnp.concatenate([_bcast_cols(q_a_norm[l]), _bcast_cols(kv_a_norm[l]),
                                _bcast_cols(q_norm_nope[l] * Q_SCALE), _bcast_cols(k_norm_nope[l]),
                                _bcast_cols(q_norm_rope[l] * Q_SCALE), _bcast_cols(k_norm_rope[l])], axis=0)
        shift = _score_bound(q_norm_nope[l], q_norm_rope[l], k_norm_nope[l], k_norm_rope[l])

        x = _ffn(x, mod, norm_w[l], ffn_w13[l, 0].astype(bf), ffn_w2[l, 0].astype(bf), 0)
        q_t, k, v_t, g0, mc = _mixer_inputs(shift, x, mod, norm_w[l], cos_t, sin_t, wa_t, wcg, wuq_t, wukv_t,
                                            colw, conv_w[l], w_branch[l, 1].astype(bf))
        o_t = _attention(shift, q_t, k, v_t)
        x = _post(x, mod, norm_w[l], o_t.reshape(b, N_HEADS * V_HEAD_DIM, s), g0, mc,
                  w_branch[l, 0].astype(bf), w_out[l].astype(bf),
                  ffn_w13[l, 1].astype(bf), ffn_w2[l, 1].astype(bf))
    return x
```

```python
import functools
import math

import jax
import jax.numpy as jnp
from jax import lax
from jax.experimental import pallas as pl
from jax.experimental.pallas import tpu as pltpu

D_MODEL = 1024
N_HEADS = 8
QK_NOPE_DIM = 64
QK_ROPE_DIM = 32
V_HEAD_DIM = 64
Q_LORA_RANK = 256
KV_LORA_RANK = 128
CONV_WIDTH = 512
CONV_K = 3
D_FF = 2816
N_SUB = 3
ROPE_THETA = 10000.0
EPS = 1e-6

LANES = 128
HEAD_PAD = 128
ROPE_HALF = QK_ROPE_DIM // 2
V_ROWS = V_HEAD_DIM + 16
A_ROWS = Q_LORA_RANK + KV_LORA_RANK + QK_ROPE_DIM
CG_COLS = 3 * CONV_WIDTH + 2 * D_MODEL
VMEM_LIMIT = 56 * 1024 * 1024

FFN_TM = 1024
FFN_SPLIT = 4
MIX_TM = 512
POST_TM = 512
POST_SPLIT = 2
ATT_TQ = 512
ATT_TK = 256
ATT_HG = 4
ATT_UNROLL = 8
ATT_LAG = 8

Q_SCALE = (QK_NOPE_DIM + QK_ROPE_DIM) ** -0.5 * math.log2(math.e)
NEG_BIG = -1e30
SHIFT_BOUND_MAX = 50.0
ROUNDING_SLACK = 1.01


def _resident(shape):
    n = len(shape)
    return pl.BlockSpec(shape, lambda *_: (0,) * n, pipeline_mode=pl.Buffered(1))


def _modulated_norm(x, nw, shift, scale):
    ms = jnp.mean(x * x, axis=-1, keepdims=True)
    return x * lax.rsqrt(ms + EPS) * (nw * (1.0 + scale)) + shift


def _rope_kernel(pos_ref, invf_ref, cos_ref, sin_ref):
    ang = pos_ref[0].astype(jnp.float32) * invf_ref[...]
    cos_ref[0] = jnp.cos(ang)
    sin_ref[0] = jnp.sin(ang)


def _rope_tables(positions):
    b, s = positions.shape
    inv_freq = 1.0 / (ROPE_THETA ** (jnp.arange(0, QK_ROPE_DIM, 2, dtype=jnp.float32) / QK_ROPE_DIM))
    invf = jnp.broadcast_to(inv_freq[:, None], (ROPE_HALF, s))
    out = jax.ShapeDtypeStruct((b, ROPE_HALF, s), jnp.float32)
    return pl.pallas_call(
        _rope_kernel,
        grid=(b,),
        in_specs=[pl.BlockSpec((1, 1, s), lambda i: (i, 0, 0)),
                  pl.BlockSpec((ROPE_HALF, s), lambda i: (0, 0))],
        out_specs=[pl.BlockSpec((1, ROPE_HALF, s), lambda i: (i, 0, 0))] * 2,
        out_shape=[out, out],
        name="rope_tables",
    )(positions.reshape(b, 1, s), invf)


def _mod_kernel(c_ref, w_ref, b_ref, o_ref):
    c = c_ref[...]
    c_act = c * jax.nn.sigmoid(c)
    o_ref[...] = jnp.dot(c_act.astype(jnp.bfloat16), w_ref[0].astype(jnp.bfloat16),
                         preferred_element_type=jnp.float32) + b_ref[0]


def _modulation(c, w_ada, b_ada, layer):
    b, d = c.shape
    depth, _, n = w_ada.shape
    tn = 1024
    return pl.pallas_call(
        _mod_kernel,
        grid=(n // tn,),
        in_specs=[pl.BlockSpec((b, d), lambda j: (0, 0)),
                  pl.BlockSpec((1, d, tn), lambda j: (layer, 0, j)),
                  pl.BlockSpec((1, 1, tn), lambda j: (layer, 0, j))],
        out_specs=pl.BlockSpec((b, tn), lambda j: (0, j)),
        out_shape=jax.ShapeDtypeStruct((b, n), jnp.float32),
        name="adaln_mod",
    )(c, w_ada, b_ada.reshape(depth, 1, n))


def _ffn_rows(xs, sub, mod_ref, nw_ref, w13_ref, w2_ref, o_ref):
    shift = mod_ref[0, 3 * sub + 0:3 * sub + 1, :]
    scale = mod_ref[0, 3 * sub + 1:3 * sub + 2, :]
    gate = 0.5 * mod_ref[0, 3 * sub + 2:3 * sub + 3, :]
    nw = nw_ref[sub:sub + 1, :]
    rows = xs[0].shape[0]
    def up(x):
        h = _modulated_norm(x, nw, shift, scale).astype(jnp.bfloat16)
        return jnp.dot(h, w13_ref[...], preferred_element_type=jnp.float32)

    gu_next = up(xs[0])
    for r, x in enumerate(xs):
        gu = gu_next
        if r + 1 < len(xs):
            gu_next = up(xs[r + 1])
        g = gu[:, :D_FF]
        u = gu[:, D_FF:]
        a = (g * jax.nn.sigmoid(g) * u).astype(jnp.bfloat16)
        y = jnp.dot(a, w2_ref[...], preferred_element_type=jnp.float32)
        o_ref[0, r * rows:(r + 1) * rows, :] = x + gate * y


def _ffn_kernel(sub, x_ref, mod_ref, nw_ref, w13_ref, w2_ref, o_ref):
    rows = x_ref.shape[1] // FFN_SPLIT
    xs = [x_ref[0, r * rows:(r + 1) * rows, :] for r in range(FFN_SPLIT)]
    _ffn_rows(xs, sub, mod_ref, nw_ref, w13_ref, w2_ref, o_ref)


def _ffn(x, mod, norm_w, w13, w2, sub):
    b, s, d = x.shape
    tm = FFN_TM
    return pl.pallas_call(
        functools.partial(_ffn_kernel, sub),
        grid=(b, s // tm),
        in_specs=[pl.BlockSpec((1, tm, d), lambda i, j: (i, j, 0)),
                  pl.BlockSpec((1, 3 * N_SUB, d), lambda i, j: (i, 0, 0)),
                  _resident(norm_w.shape),
                  _resident(w13.shape),
                  _resident(w2.shape)],
        out_specs=pl.BlockSpec((1, tm, d), lambda i, j: (i, j, 0)),
        out_shape=jax.ShapeDtypeStruct(x.shape, jnp.float32),
        compiler_params=pltpu.CompilerParams(
            dimension_semantics=("arbitrary", "arbitrary"), vmem_limit_bytes=VMEM_LIMIT),
        name=f"ffn{sub}",
    )(x, mod, norm_w, w13, w2)


def _rms_rows(x, w):
    ms = jnp.mean(x * x, axis=0, keepdims=True)
    return x * lax.rsqrt(ms + EPS) * w


def _rope_rows(y, cos, sin):
    y1 = y[:ROPE_HALF]
    y2 = y[ROPE_HALF:]
    return y1 * cos - y2 * sin, y2 * cos + y1 * sin


def _mix_kernel(shift_ref, x_ref, mod_ref, nw_ref, cos_ref, sin_ref, wa_ref, wcg_ref, wuq_ref, wukv_ref,
                colw_ref, convw_ref, wb1_ref,
                q_ref, k_ref, v_ref, g0_ref, mc_ref, tail_ref):
    tm = x_ref.shape[1]
    rep = tm // LANES
    j = pl.program_id(1)

    @pl.when(j == 0)
    def _():
        tail_ref[...] = jnp.zeros_like(tail_ref)

    half = tm // 2
    hs = [_modulated_norm(x_ref[0, r * half:(r + 1) * half, :], nw_ref[1:2, :], mod_ref[0, 3:4, :],
                          mod_ref[0, 4:5, :]).astype(jnp.bfloat16) for r in range(2)]

    def rows_dot(w):
        return jnp.concatenate([jnp.dot(h, w, preferred_element_type=jnp.float32) for h in hs], axis=0)

    zt = jnp.concatenate([lax.dot_general(wa_ref[...], h, (((1,), (1,)), ((), ())),
                                          preferred_element_type=jnp.float32) for h in hs],
                         axis=1)
    cv = rows_dot(wcg_ref[:, 0:3 * CONV_WIDTH])

    def colw(lo, n):
        return jnp.tile(colw_ref[lo:lo + n, :], (1, rep))

    o_qa, o_kva = 0, Q_LORA_RANK
    o_qn = o_kva + KV_LORA_RANK
    o_kn = o_qn + QK_NOPE_DIM
    o_qr = o_kn + QK_NOPE_DIM
    o_kr = o_qr + QK_ROPE_DIM

    cq = _rms_rows(zt[0:Q_LORA_RANK], colw(o_qa, Q_LORA_RANK)).astype(jnp.bfloat16)
    ckv = _rms_rows(zt[Q_LORA_RANK:Q_LORA_RANK + KV_LORA_RANK], colw(o_kva, KV_LORA_RANK)).astype(jnp.bfloat16)
    kr_raw = zt[Q_LORA_RANK + KV_LORA_RANK:A_ROWS]
    cos = cos_ref[0]
    sin = sin_ref[0]
    kr1, kr2 = _rope_rows(_rms_rows(kr_raw, colw(o_kr, QK_ROPE_DIM)), cos, sin)

    gl0 = rows_dot(wcg_ref[:, 3 * CONV_WIDTH:3 * CONV_WIDTH + D_MODEL])
    qt = jnp.dot(wuq_ref[...], cq, preferred_element_type=jnp.float32)
    kvt = jnp.dot(wukv_ref[...], ckv, preferred_element_type=jnp.float32)
    gl1 = rows_dot(wcg_ref[:, 3 * CONV_WIDTH + D_MODEL:CG_COLS])

    xv = cv[:, 0:CONV_WIDTH]
    gate_b = cv[:, CONV_WIDTH:2 * CONV_WIDTH]
    gate_c = cv[:, 2 * CONV_WIDTH:3 * CONV_WIDTH]
    u = gate_c * xv

    p1 = tail_ref[7:8, :]
    p2 = tail_ref[6:7, :]
    row = lax.broadcasted_iota(jnp.int32, u.shape, 0)
    r1 = jnp.where(row == 0, p1, pltpu.roll(u, 1, 0))
    r2 = jnp.where(row == 0, p2, jnp.where(row == 1, p1, pltpu.roll(u, 2, 0)))
    tail_ref[...] = u[tm - 8:tm, :]
    conv = convw_ref[0:1, :] * r2 + convw_ref[1:2, :] * r1 + convw_ref[2:3, :] * u
    o_conv = (gate_b * conv).astype(jnp.bfloat16)
    y_conv = jnp.dot(o_conv, wb1_ref[...], preferred_element_type=jnp.float32)
    g0_ref[0] = jax.nn.sigmoid(gl0).astype(jnp.bfloat16)
    mc_ref[0] = (jax.nn.sigmoid(gl1) * y_conv).astype(jnp.bfloat16)

    wqn = colw(o_qn, QK_NOPE_DIM)
    wkn = colw(o_kn, QK_NOPE_DIM)
    wqr = colw(o_qr, QK_ROPE_DIM)
    shift = shift_ref[0]
    folded = jnp.where(shift <= SHIFT_BOUND_MAX, shift, 0.0)
    pad_row = lax.broadcasted_iota(jnp.int32, (HEAD_PAD - QK_NOPE_DIM - QK_ROPE_DIM, tm), 0)
    k_pad = jnp.where(pad_row == 0, 1.0, 0.0)
    q_pad = jnp.where(pad_row == 0, -folded, 0.0).astype(jnp.bfloat16)
    ones = jnp.ones((V_ROWS - V_HEAD_DIM, tm), jnp.bfloat16)
    for hd in range(N_HEADS):
        base = hd * HEAD_PAD
        qn = _rms_rows(qt[base:base + QK_NOPE_DIM], wqn)
        qr1, qr2 = _rope_rows(
            _rms_rows(qt[base + QK_NOPE_DIM:base + QK_NOPE_DIM + QK_ROPE_DIM], wqr), cos, sin)
        q_ref[0, hd, 0:QK_NOPE_DIM, :] = qn.astype(jnp.bfloat16)
        q_ref[0, hd, QK_NOPE_DIM:QK_NOPE_DIM + ROPE_HALF, :] = qr1.astype(jnp.bfloat16)
        q_ref[0, hd, QK_NOPE_DIM + ROPE_HALF:QK_NOPE_DIM + QK_ROPE_DIM, :] = qr2.astype(jnp.bfloat16)
        q_ref[0, hd, QK_NOPE_DIM + QK_ROPE_DIM:HEAD_PAD, :] = q_pad

        kn = _rms_rows(kvt[base:base + QK_NOPE_DIM], wkn)
        kt = jnp.concatenate([kn, kr1, kr2, k_pad], axis=0)
        k_ref[0, hd] = kt.T.astype(jnp.bfloat16)

        v_ref[0, hd, 0:V_HEAD_DIM, :] = kvt[base + QK_NOPE_DIM:base + HEAD_PAD].astype(jnp.bfloat16)
        v_ref[0, hd, V_HEAD_DIM:V_ROWS, :] = ones


def _mixer_inputs(shift, x, mod, norm_w, cos_t, sin_t, wa_t, wcg, wuq_t, wukv_t, colw, conv_w, wb1):
    b, s, d = x.shape
    tm = MIX_TM
    bf = jnp.bfloat16
    out_shape = [
        jax.ShapeDtypeStruct((b, N_HEADS, HEAD_PAD, s), bf),
        jax.ShapeDtypeStruct((b, N_HEADS, s, HEAD_PAD), bf),
        jax.ShapeDtypeStruct((b, N_HEADS, V_ROWS, s), bf),
        jax.ShapeDtypeStruct((b, s, d), bf),
        jax.ShapeDtypeStruct((b, s, d), bf),
    ]
    out_specs = [
        pl.BlockSpec((1, N_HEADS, HEAD_PAD, tm), lambda i, j: (i, 0, 0, j)),
        pl.BlockSpec((1, N_HEADS, tm, HEAD_PAD), lambda i, j: (i, 0, j, 0)),
        pl.BlockSpec((1, N_HEADS, V_ROWS, tm), lambda i, j: (i, 0, 0, j)),
        pl.BlockSpec((1, tm, d), lambda i, j: (i, j, 0)),
        pl.BlockSpec((1, tm, d), lambda i, j: (i, j, 0)),
    ]
    in_specs = [
        pl.BlockSpec(memory_space=pltpu.SMEM),
        pl.BlockSpec((1, tm, d), lambda i, j: (i, j, 0)),
        pl.BlockSpec((1, 3 * N_SUB, d), lambda i, j: (i, 0, 0)),
        _resident(norm_w.shape),
        pl.BlockSpec((1, ROPE_HALF, tm), lambda i, j: (i, 0, j)),
        pl.BlockSpec((1, ROPE_HALF, tm), lambda i, j: (i, 0, j)),
        _resident(wa_t.shape), _resident(wcg.shape), _resident(wuq_t.shape), _resident(wukv_t.shape),
        _resident(colw.shape), _resident(conv_w.shape), _resident(wb1.shape),
    ]
    return pl.pallas_call(
        _mix_kernel,
        grid=(b, s // tm),
        in_specs=in_specs,
        out_specs=out_specs,
        out_shape=out_shape,
        scratch_shapes=[pltpu.VMEM((8, CONV_WIDTH), jnp.float32)],
        compiler_params=pltpu.CompilerParams(
            dimension_semantics=("arbitrary", "arbitrary"), vmem_limit_bytes=VMEM_LIMIT),
        name="mixer_inputs",
    )(shift, x, mod, norm_w, cos_t, sin_t, wa_t, wcg, wuq_t, wukv_t, colw, conv_w, wb1)


def _attn_kernel(shift_ref, q_ref, k_ref, v_ref, o_ref, m_ref, acc_ref):
    tq = q_ref.shape[3]
    tk = ATT_TK
    i = pl.program_id(2)
    diag = pl.multiple_of(i * tq, tq)
    n_chunks = i * (tq // tk)
    shift = shift_ref[0]

    def causal_mask(rows, cols):
        key = lax.broadcasted_iota(jnp.int32, (rows, cols), 0)
        qry = lax.broadcasted_iota(jnp.int32, (rows, cols), 1)
        return key <= qry

    n_slabs = tq // tk

    def shifted_chunks(chunks):
        items = [(c, hd) for c in range(len(chunks)) for hd in range(ATT_HG)]
        pending = {}
        out = [[None] * n_slabs for _ in range(ATT_HG)]
        for n in range(len(items) + ATT_LAG):
            if n < len(items):
                c, hd = items[n]
                start, mask, slab = chunks[c]
                s = jnp.dot(k_ref[0, hd, pl.ds(start, tk), :], q_ref[0, hd, :, slab * tk:(slab + 1) * tk],
                            preferred_element_type=jnp.float32)
                pending[n] = s if mask is None else jnp.where(mask, s, NEG_BIG)
            if n >= ATT_LAG:
                c, hd = items[n - ATT_LAG]
                start, _, slab = chunks[c]
                p = jnp.exp2(pending.pop(n - ATT_LAG)).astype(jnp.bfloat16)
                pv = jnp.dot(v_ref[0, hd, :, pl.ds(start, tk)], p, preferred_element_type=jnp.float32)
                out[hd][slab] = pv if out[hd][slab] is None else out[hd][slab] + pv
        return out

    def full_chunk(start):
        return [(start, None, slab) for slab in range(n_slabs)]

    diag_chunks = []
    for d in range(n_slabs):
        start = pl.multiple_of(diag + d * tk, tk)
        diag_chunks.append((start, causal_mask(tk, tk), d))
        diag_chunks += [(start, None, slab) for slab in range(d + 1, n_slabs)]

    @pl.when(shift <= SHIFT_BOUND_MAX)
    def _():
        n_groups = n_chunks // ATT_UNROLL
        n_rest = n_chunks - n_groups * ATT_UNROLL

        for rest in range(0, ATT_UNROLL, tq // tk):
            @pl.when(n_rest == rest)
            def _():
                first = n_groups * (ATT_UNROLL * tk)
                chunks = []
                for u in range(rest):
                    chunks += full_chunk(pl.multiple_of(first + u * tk, tk))
                for hd, slabs in enumerate(shifted_chunks(chunks + diag_chunks)):
                    for slab, pv in enumerate(slabs):
                        acc_ref[hd, :, slab * tk:(slab + 1) * tk] = pv

        def group_body(g, carry):
            base = g * (ATT_UNROLL * tk)
            chunks = []
            for u in range(ATT_UNROLL):
                chunks += full_chunk(pl.multiple_of(base + u * tk, tk))
            for hd, slabs in enumerate(shifted_chunks(chunks)):
                for slab, pv in enumerate(slabs):
                    acc_ref[hd, :, slab * tk:(slab + 1) * tk] += pv
            return carry

        lax.fori_loop(0, n_groups, group_body, 0)

    @pl.when(shift > SHIFT_BOUND_MAX)
    def _():
        def scores(start, n):
            return [jnp.dot(k_ref[0, hd, pl.ds(start, n), :], q_ref[0, hd],
                            preferred_element_type=jnp.float32) for hd in range(ATT_HG)]

        visible = causal_mask(tq, tq)
        for hd, s in enumerate(scores(diag, tq)):
            s = jnp.where(visible, s, NEG_BIG)
            m = jnp.max(s, axis=0, keepdims=True)
            p = jnp.exp2(s - m).astype(jnp.bfloat16)
            acc_ref[hd] = jnp.dot(v_ref[0, hd, :, pl.ds(diag, tq)], p, preferred_element_type=jnp.float32)
            m_ref[hd] = m

        def body(j, carry):
            start = pl.multiple_of(j * tk, tk)
            for hd, s in enumerate(scores(start, tk)):
                m_old = m_ref[hd]
                m_new = jnp.maximum(m_old, jnp.max(s, axis=0, keepdims=True))
                p = jnp.exp2(s - m_new).astype(jnp.bfloat16)
                alpha = jnp.exp2(m_old - m_new)
                pv = jnp.dot(v_ref[0, hd, :, pl.ds(start, tk)], p, preferred_element_type=jnp.float32)
                acc_ref[hd] = alpha * acc_ref[hd] + pv
                m_ref[hd] = m_new
            return carry

        lax.fori_loop(0, n_chunks, body, 0)

    for hd in range(ATT_HG):
        acc = acc_ref[hd]
        o_ref[0, hd] = (acc[0:V_HEAD_DIM] / acc[V_HEAD_DIM:V_HEAD_DIM + 1]).astype(o_ref.dtype)


def _attention(shift, q_t, k, v_t):
    b, nh, _, s = q_t.shape
    tq = ATT_TQ
    hg = ATT_HG
    return pl.pallas_call(
        _attn_kernel,
        grid=(b, nh // hg, s // tq),
        in_specs=[pl.BlockSpec(memory_space=pltpu.SMEM),
                  pl.BlockSpec((1, hg, HEAD_PAD, tq), lambda bi, g, i: (bi, g, 0, i)),
                  pl.BlockSpec((1, hg, s, HEAD_PAD), lambda bi, g, i: (bi, g, 0, 0)),
                  pl.BlockSpec((1, hg, V_ROWS, s), lambda bi, g, i: (bi, g, 0, 0))],
        out_specs=pl.BlockSpec((1, hg, V_HEAD_DIM, tq), lambda bi, g, i: (bi, g, 0, i)),
        out_shape=jax.ShapeDtypeStruct((b, nh, V_HEAD_DIM, s), jnp.bfloat16),
        scratch_shapes=[pltpu.VMEM((hg, 1, tq), jnp.float32),
                        pltpu.VMEM((hg, V_ROWS, tq), jnp.float32)],
        compiler_params=pltpu.CompilerParams(
            dimension_semantics=("arbitrary", "arbitrary", "arbitrary"), vmem_limit_bytes=VMEM_LIMIT),
        name="mla_attention",
    )(shift, q_t, k, v_t)


def _post_kernel(x_ref, mod_ref, nw_ref, o_ref, g0_ref, mc_ref, wb0_ref, wout_ref, w13_ref, w2_ref, y_ref):
    rows = x_ref.shape[1] // POST_SPLIT
    gate = mod_ref[0, 5:6, :]
    xs = []
    for r in range(POST_SPLIT):
        sl = slice(r * rows, (r + 1) * rows)
        y_attn = lax.dot_general(o_ref[0, :, sl], wb0_ref[...], (((0,), (0,)), ((), ())),
                                 preferred_element_type=jnp.float32)
        merged = g0_ref[0, sl, :].astype(jnp.float32) * y_attn + mc_ref[0, sl, :].astype(jnp.float32)
        y = jnp.dot(merged.astype(jnp.bfloat16), wout_ref[...], preferred_element_type=jnp.float32)
        xs.append(x_ref[0, sl, :] + gate * y)
    _ffn_rows(xs, 2, mod_ref, nw_ref, w13_ref, w2_ref, y_ref)


def _post(x, mod, norm_w, o_t, g0, mc, wb0, wout, w13, w2):
    b, s, d = x.shape
    tm = POST_TM
    rows = o_t.shape[1]
    return pl.pallas_call(
        _post_kernel,
        grid=(b, s // tm),
        in_specs=[pl.BlockSpec((1, tm, d), lambda i, j: (i, j, 0)),
                  pl.BlockSpec((1, 3 * N_SUB, d), lambda i, j: (i, 0, 0)),
                  _resident(norm_w.shape),
                  pl.BlockSpec((1, rows, tm), lambda i, j: (i, 0, j)),
                  pl.BlockSpec((1, tm, d), lambda i, j: (i, j, 0)),
                  pl.BlockSpec((1, tm, d), lambda i, j: (i, j, 0)),
                  _resident(wb0.shape), _resident(wout.shape), _resident(w13.shape), _resident(w2.shape)],
        out_specs=pl.BlockSpec((1, tm, d), lambda i, j: (i, j, 0)),
        out_shape=jax.ShapeDtypeStruct(x.shape, jnp.float32),
        compiler_params=pltpu.CompilerParams(
            dimension_semantics=("arbitrary", "arbitrary"), vmem_limit_bytes=VMEM_LIMIT),
        name="merge_out_ffn2",
    )(x, mod, norm_w, o_t, g0, mc, wb0, wout, w13, w2)


def _score_bound(wqn, wqr, wkn, wkr):
    def sq(w_nope, w_rope):
        return QK_NOPE_DIM * jnp.max(jnp.square(w_nope)) + QK_ROPE_DIM * jnp.max(jnp.square(w_rope))
    bound = Q_SCALE * jnp.sqrt(sq(wqn, wqr) * sq(wkn, wkr)) * ROUNDING_SLACK
    return bound.reshape(1).astype(jnp.float32)


def _bcast_cols(v):
    return jnp.broadcast_to(v[:, None], (v.shape[0], LANES))


def kernel(x, c, positions, w_ada, b_ada, norm_w, ffn_w13, ffn_w2, w_in, q_a_norm, w_uq, kv_a_norm,
           w_ukv, q_norm_nope, k_norm_nope, q_norm_rope, k_norm_rope, conv_w, w_branch, w_out):
    bf = jnp.bfloat16
    b, s, d = x.shape
    depth = w_ada.shape[0]
    cos_t, sin_t = _rope_tables(positions)
    for l in range(depth):
        mod = _modulation(c, w_ada, b_ada, l).reshape(b, 3 * N_SUB, d)

        wi = w_in[l]
        wa_t = wi[:, :A_ROWS].T.astype(bf)
        wcg = wi[:, A_ROWS:].astype(bf)
        wuq = w_uq[l].reshape(Q_LORA_RANK, N_HEADS, QK_NOPE_DIM + QK_ROPE_DIM)
        wuq = jnp.pad(wuq, ((0, 0), (0, 0), (0, HEAD_PAD - QK_NOPE_DIM - QK_ROPE_DIM)))
        wuq_t = wuq.reshape(Q_LORA_RANK, N_HEADS * HEAD_PAD).T.astype(bf)
        wukv_t = w_ukv[l].T.astype(bf)
        colw = jnp.concatenate([_bcast_cols(q_a_norm[l]), _bcast_cols(kv_a_norm[l]),
                                _bcast_cols(q_norm_nope[l] * Q_SCALE), _bcast_cols(k_norm_nope[l]),
                                _bcast_cols(q_norm_rope[l] * Q_SCALE), _bcast_cols(k_norm_rope[l])], axis=0)
        shift = _score_bound(q_norm_nope[l], q_norm_rope[l], k_norm_nope[l], k_norm_rope[l])

        x = _ffn(x, mod, norm_w[l], ffn_w13[l, 0].astype(bf), ffn_w2[l, 0].astype(bf), 0)
        q_t, k, v_t, g0, mc = _mixer_inputs(shift, x, mod, norm_w[l], cos_t, sin_t, wa_t, wcg, wuq_t, wukv_t,
                                            colw, conv_w[l], w_branch[l, 1].astype(bf))
        o_t = _attention(shift, q_t, k, v_t)
        x = _post(x, mod, norm_w[l], o_t.reshape(b, N_HEADS * V_HEAD_DIM, s), g0, mc,
                  w_branch[l, 0].astype(bf), w_out[l].astype(bf),
                  ffn_w13[l, 1].astype(bf), ffn_w2[l, 1].astype(bf))
    return x
```

```python
import functools
import math

import jax
import jax.numpy as jnp
from jax import lax
from jax.experimental import pallas as pl
from jax.experimental.pallas import tpu as pltpu

D_MODEL = 1024
N_HEADS = 8
QK_NOPE_DIM = 64
QK_ROPE_DIM = 32
V_HEAD_DIM = 64
Q_LORA_RANK = 256
KV_LORA_RANK = 128
CONV_WIDTH = 512
CONV_K = 3
D_FF = 2816
N_SUB = 3
ROPE_THETA = 10000.0
EPS = 1e-6

LANES = 128
HEAD_PAD = 128
ROPE_HALF = QK_ROPE_DIM // 2
V_ROWS = V_HEAD_DIM + 16
A_ROWS = Q_LORA_RANK + KV_LORA_RANK + QK_ROPE_DIM
CG_COLS = 3 * CONV_WIDTH + 2 * D_MODEL
VMEM_LIMIT = 56 * 1024 * 1024

FFN_TM = 1024
FFN_SPLIT = 4
MIX_TM = 1024
POST_TM = 1024
POST_SPLIT = 4
ATT_TQ = 1024
ATT_TK = 256
ATT_HG = 4
ATT_UNROLL = 8
ATT_LAG = 8

Q_SCALE = (QK_NOPE_DIM + QK_ROPE_DIM) ** -0.5 * math.log2(math.e)
NEG_BIG = -1e30
SHIFT_BOUND_MAX = 50.0
ROUNDING_SLACK = 1.01


def _resident(shape):
    n = len(shape)
    return pl.BlockSpec(shape, lambda *_: (0,) * n, pipeline_mode=pl.Buffered(1))


def _modulated_norm(x, nw, shift, scale):
    ms = jnp.mean(x * x, axis=-1, keepdims=True)
    return x * lax.rsqrt(ms + EPS) * (nw * (1.0 + scale)) + shift


def _rope_kernel(pos_ref, invf_ref, cos_ref, sin_ref):
    ang = pos_ref[0].astype(jnp.float32) * invf_ref[...]
    cos_ref[0] = jnp.cos(ang)
    sin_ref[0] = jnp.sin(ang)


def _rope_tables(positions):
    b, s = positions.shape
    inv_freq = 1.0 / (ROPE_THETA ** (jnp.arange(0, QK_ROPE_DIM, 2, dtype=jnp.float32) / QK_ROPE_DIM))
    invf = jnp.broadcast_to(inv_freq[:, None], (ROPE_HALF, s))
    out = jax.ShapeDtypeStruct((b, ROPE_HALF, s), jnp.float32)
    return pl.pallas_call(
        _rope_kernel,
        grid=(b,),
        in_specs=[pl.BlockSpec((1, 1, s), lambda i: (i, 0, 0)),
                  pl.BlockSpec((ROPE_HALF, s), lambda i: (0, 0))],
        out_specs=[pl.BlockSpec((1, ROPE_HALF, s), lambda i: (i, 0, 0))] * 2,
        out_shape=[out, out],
        name="rope_tables",
    )(positions.reshape(b, 1, s), invf)


def _mod_kernel(c_ref, w_ref, b_ref, o_ref):
    c = c_ref[...]
    c_act = c * jax.nn.sigmoid(c)
    o_ref[...] = jnp.dot(c_act.astype(jnp.bfloat16), w_ref[0].astype(jnp.bfloat16),
                         preferred_element_type=jnp.float32) + b_ref[0]


def _modulation(c, w_ada, b_ada, layer):
    b, d = c.shape
    depth, _, n = w_ada.shape
    tn = 1024
    return pl.pallas_call(
        _mod_kernel,
        grid=(n // tn,),
        in_specs=[pl.BlockSpec((b, d), lambda j: (0, 0)),
                  pl.BlockSpec((1, d, tn), lambda j: (layer, 0, j)),
                  pl.BlockSpec((1, 1, tn), lambda j: (layer, 0, j))],
        out_specs=pl.BlockSpec((b, tn), lambda j: (0, j)),
        out_shape=jax.ShapeDtypeStruct((b, n), jnp.float32),
        name="adaln_mod",
    )(c, w_ada, b_ada.reshape(depth, 1, n))


def _ffn_rows(xs, sub, mod_ref, nw_ref, w13_ref, w2_ref, o_ref):
    shift = mod_ref[0, 3 * sub + 0:3 * sub + 1, :]
    scale = mod_ref[0, 3 * sub + 1:3 * sub + 2, :]
    gate = 0.5 * mod_ref[0, 3 * sub + 2:3 * sub + 3, :]
    nw = nw_ref[sub:sub + 1, :]
    rows = xs[0].shape[0]
    def up(x):
        h = _modulated_norm(x, nw, shift, scale).astype(jnp.bfloat16)
        return jnp.dot(h, w13_ref[...], preferred_element_type=jnp.float32)

    gu_next = up(xs[0])
    for r, x in enumerate(xs):
        gu = gu_next
        if r + 1 < len(xs):
            gu_next = up(xs[r + 1])
        g = gu[:, :D_FF]
        u = gu[:, D_FF:]
        a = (g * jax.nn.sigmoid(g) * u).astype(jnp.bfloat16)
        y = jnp.dot(a, w2_ref[...], preferred_element_type=jnp.float32)
        o_ref[0, r * rows:(r + 1) * rows, :] = x + gate * y


def _ffn_kernel(sub, x_ref, mod_ref, nw_ref, w13_ref, w2_ref, o_ref):
    rows = x_ref.shape[1] // FFN_SPLIT
    xs = [x_ref[0, r * rows:(r + 1) * rows, :] for r in range(FFN_SPLIT)]
    _ffn_rows(xs, sub, mod_ref, nw_ref, w13_ref, w2_ref, o_ref)


def _ffn(x, mod, norm_w, w13, w2, sub):
    b, s, d = x.shape
    tm = FFN_TM
    return pl.pallas_call(
        functools.partial(_ffn_kernel, sub),
        grid=(b, s // tm),
        in_specs=[pl.BlockSpec((1, tm, d), lambda i, j: (i, j, 0)),
                  pl.BlockSpec((1, 3 * N_SUB, d), lambda i, j: (i, 0, 0)),
                  _resident(norm_w.shape),
                  _resident(w13.shape),
                  _resident(w2.shape)],
        out_specs=pl.BlockSpec((1, tm, d), lambda i, j: (i, j, 0)),
        out_shape=jax.ShapeDtypeStruct(x.shape, jnp.float32),
        compiler_params=pltpu.CompilerParams(
            dimension_semantics=("arbitrary", "arbitrary"), vmem_limit_bytes=VMEM_LIMIT),
        name=f"ffn{sub}",
    )(x, mod, norm_w, w13, w2)


def _rms_rows(x, w):
    ms = jnp.mean(x * x, axis=0, keepdims=True)
    return x * lax.rsqrt(ms + EPS) * w


def _rope_rows(y, cos, sin):
    y1 = y[:ROPE_HALF]
    y2 = y[ROPE_HALF:]
    return y1 * cos - y2 * sin, y2 * cos + y1 * sin


def _mix_kernel(shift_ref, x_ref, mod_ref, nw_ref, cos_ref, sin_ref, wa_ref, wcg_ref, wuq_ref, wukv_ref,
                colw_ref, convw_ref, wb1_ref,
                q_ref, k_ref, v_ref, g0_ref, mc_ref, tail_ref):
    tm = x_ref.shape[1]
    rep = tm // LANES
    j = pl.program_id(1)

    @pl.when(j == 0)
    def _():
        tail_ref[...] = jnp.zeros_like(tail_ref)

    half = tm // 2
    hs = [_modulated_norm(x_ref[0, r * half:(r + 1) * half, :], nw_ref[1:2, :], mod_ref[0, 3:4, :],
                          mod_ref[0, 4:5, :]).astype(jnp.bfloat16) for r in range(2)]

    def rows_dot(w):
        return jnp.concatenate([jnp.dot(h, w, preferred_element_type=jnp.float32) for h in hs], axis=0)

    zt = jnp.concatenate([lax.dot_general(wa_ref[...], h, (((1,), (1,)), ((), ())),
                                          preferred_element_type=jnp.float32) for h in hs],
                         axis=1)
    cv = rows_dot(wcg_ref[:, 0:3 * CONV_WIDTH])

    def colw(lo, n):
        return jnp.tile(colw_ref[lo:lo + n, :], (1, rep))

    o_qa, o_kva = 0, Q_LORA_RANK
    o_qn = o_kva + KV_LORA_RANK
    o_kn = o_qn + QK_NOPE_DIM
    o_qr = o_kn + QK_NOPE_DIM
    o_kr = o_qr + QK_ROPE_DIM

    cq = _rms_rows(zt[0:Q_LORA_RANK], colw(o_qa, Q_LORA_RANK)).astype(jnp.bfloat16)
    ckv = _rms_rows(zt[Q_LORA_RANK:Q_LORA_RANK + KV_LORA_RANK], colw(o_kva, KV_LORA_RANK)).astype(jnp.bfloat16)
    kr_raw = zt[Q_LORA_RANK + KV_LORA_RANK:A_ROWS]
    cos = cos_ref[0]
    sin = sin_ref[0]
    kr1, kr2 = _rope_rows(_rms_rows(kr_raw, colw(o_kr, QK_ROPE_DIM)), cos, sin)

    gl0 = rows_dot(wcg_ref[:, 3 * CONV_WIDTH:3 * CONV_WIDTH + D_MODEL])
    qt = jnp.dot(wuq_ref[...], cq, preferred_element_type=jnp.float32)
    kvt = jnp.dot(wukv_ref[...], ckv, preferred_element_type=jnp.float32)
    gl1 = rows_dot(wcg_ref[:, 3 * CONV_WIDTH + D_MODEL:CG_COLS])

    xv = cv[:, 0:CONV_WIDTH]
    gate_b = cv[:, CONV_WIDTH:2 * CONV_WIDTH]
    gate_c = cv[:, 2 * CONV_WIDTH:3 * CONV_WIDTH]
    u = gate_c * xv

    p1 = tail_ref[7:8, :]
    p2 = tail_ref[6:7, :]
    row = lax.broadcasted_iota(jnp.int32, u.shape, 0)
    r1 = jnp.where(row == 0, p1, pltpu.roll(u, 1, 0))
    r2 = jnp.where(row == 0, p2, jnp.where(row == 1, p1, pltpu.roll(u, 2, 0)))
    tail_ref[...] = u[tm - 8:tm, :]
    conv = convw_ref[0:1, :] * r2 + convw_ref[1:2, :] * r1 + convw_ref[2:3, :] * u
    o_conv = (gate_b * conv).astype(jnp.bfloat16)
    y_conv = jnp.dot(o_conv, wb1_ref[...], preferred_element_type=jnp.float32)
    g0_ref[0] = jax.nn.sigmoid(gl0).astype(jnp.bfloat16)
    mc_ref[0] = (jax.nn.sigmoid(gl1) * y_conv).astype(jnp.bfloat16)

    wqn = colw(o_qn, QK_NOPE_DIM)
    wkn = colw(o_kn, QK_NOPE_DIM)
    wqr = colw(o_qr, QK_ROPE_DIM)
    shift = shift_ref[0]
    folded = jnp.where(shift <= SHIFT_BOUND_MAX, shift, 0.0)
    pad_row = lax.broadcasted_iota(jnp.int32, (HEAD_PAD - QK_NOPE_DIM - QK_ROPE_DIM, tm), 0)
    k_pad = jnp.where(pad_row == 0, 1.0, 0.0)
    q_pad = jnp.where(pad_row == 0, -folded, 0.0).astype(jnp.bfloat16)
    ones = jnp.ones((V_ROWS - V_HEAD_DIM, tm), jnp.bfloat16)
    for hd in range(N_HEADS):
        base = hd * HEAD_PAD
        qn = _rms_rows(qt[base:base + QK_NOPE_DIM], wqn)
        qr1, qr2 = _rope_rows(
            _rms_rows(qt[base + QK_NOPE_DIM:base + QK_NOPE_DIM + QK_ROPE_DIM], wqr), cos, sin)
        q_ref[0, hd, 0:QK_NOPE_DIM, :] = qn.astype(jnp.bfloat16)
        q_ref[0, hd, QK_NOPE_DIM:QK_NOPE_DIM + ROPE_HALF, :] = qr1.astype(jnp.bfloat16)
        q_ref[0, hd, QK_NOPE_DIM + ROPE_HALF:QK_NOPE_DIM + QK_ROPE_DIM, :] = qr2.astype(jnp.bfloat16)
        q_ref[0, hd, QK_NOPE_DIM + QK_ROPE_DIM:HEAD_PAD, :] = q_pad

        kn = _rms_rows(kvt[base:base + QK_NOPE_DIM], wkn)
        kt = jnp.concatenate([kn, kr1, kr2, k_pad], axis=0)
        k_ref[0, hd] = kt.T.astype(jnp.bfloat16)

        v_ref[0, hd, 0:V_HEAD_DIM, :] = kvt[base + QK_NOPE_DIM:base + HEAD_PAD].astype(jnp.bfloat16)
        v_ref[0, hd, V_HEAD_DIM:V_ROWS, :] = ones


def _mixer_inputs(shift, x, mod, norm_w, cos_t, sin_t, wa_t, wcg, wuq_t, wukv_t, colw, conv_w, wb1):
    b, s, d = x.shape
    tm = MIX_TM
    bf = jnp.bfloat16
    out_shape = [
        jax.ShapeDtypeStruct((b, N_HEADS, HEAD_PAD, s), bf),
        jax.ShapeDtypeStruct((b, N_HEADS, s, HEAD_PAD), bf),
        jax.ShapeDtypeStruct((b, N_HEADS, V_ROWS, s), bf),
        jax.ShapeDtypeStruct((b, s, d), bf),
        jax.ShapeDtypeStruct((b, s, d), bf),
    ]
    out_specs = [
        pl.BlockSpec((1, N_HEADS, HEAD_PAD, tm), lambda i, j: (i, 0, 0, j)),
        pl.BlockSpec((1, N_HEADS, tm, HEAD_PAD), lambda i, j: (i, 0, j, 0)),
        pl.BlockSpec((1, N_HEADS, V_ROWS, tm), lambda i, j: (i, 0, 0, j)),
        pl.BlockSpec((1, tm, d), lambda i, j: (i, j, 0)),
        pl.BlockSpec((1, tm, d), lambda i, j: (i, j, 0)),
    ]
    in_specs = [
        pl.BlockSpec(memory_space=pltpu.SMEM),
        pl.BlockSpec((1, tm, d), lambda i, j: (i, j, 0)),
        pl.BlockSpec((1, 3 * N_SUB, d), lambda i, j: (i, 0, 0)),
        _resident(norm_w.shape),
        pl.BlockSpec((1, ROPE_HALF, tm), lambda i, j: (i, 0, j)),
        pl.BlockSpec((1, ROPE_HALF, tm), lambda i, j: (i, 0, j)),
        _resident(wa_t.shape), _resident(wcg.shape), _resident(wuq_t.shape), _resident(wukv_t.shape),
        _resident(colw.shape), _resident(conv_w.shape), _resident(wb1.shape),
    ]
    return pl.pallas_call(
        _mix_kernel,
        grid=(b, s // tm),
        in_specs=in_specs,
        out_specs=out_specs,
        out_shape=out_shape,
        scratch_shapes=[pltpu.VMEM((8, CONV_WIDTH), jnp.float32)],
        compiler_params=pltpu.CompilerParams(
            dimension_semantics=("arbitrary", "arbitrary"), vmem_limit_bytes=VMEM_LIMIT),
        name="mixer_inputs",
    )(shift, x, mod, norm_w, cos_t, sin_t, wa_t, wcg, wuq_t, wukv_t, colw, conv_w, wb1)


def _attn_kernel(shift_ref, q_ref, k_ref, v_ref, o_ref, m_ref, acc_ref):
    tq = q_ref.shape[3]
    tk = ATT_TK
    i = pl.program_id(2)
    diag = pl.multiple_of(i * tq, tq)
    n_chunks = i * (tq // tk)
    shift = shift_ref[0]

    def causal_mask(rows, cols):
        key = lax.broadcasted_iota(jnp.int32, (rows, cols), 0)
        qry = lax.broadcasted_iota(jnp.int32, (rows, cols), 1)
        return key <= qry

    n_slabs = tq // tk

    def shifted_chunks(chunks):
        items = [(c, hd) for c in range(len(chunks)) for hd in range(ATT_HG)]
        pending = {}
        out = [[None] * n_slabs for _ in range(ATT_HG)]
        for n in range(len(items) + ATT_LAG):
            if n < len(items):
                c, hd = items[n]
                start, mask, slab = chunks[c]
                s = jnp.dot(k_ref[0, hd, pl.ds(start, tk), :], q_ref[0, hd, :, slab * tk:(slab + 1) * tk],
                            preferred_element_type=jnp.float32)
                pending[n] = s if mask is None else jnp.where(mask, s, NEG_BIG)
            if n >= ATT_LAG:
                c, hd = items[n - ATT_LAG]
                start, _, slab = chunks[c]
                p = jnp.exp2(pending.pop(n - ATT_LAG)).astype(jnp.bfloat16)
                pv = jnp.dot(v_ref[0, hd, :, pl.ds(start, tk)], p, preferred_element_type=jnp.float32)
                out[hd][slab] = pv if out[hd][slab] is None else out[hd][slab] + pv
        return out

    def full_chunk(start):
        return [(start, None, slab) for slab in range(n_slabs)]

    diag_chunks = []
    for d in range(n_slabs):
        start = pl.multiple_of(diag + d * tk, tk)
        diag_chunks.append((start, causal_mask(tk, tk), d))
        diag_chunks += [(start, None, slab) for slab in range(d + 1, n_slabs)]

    @pl.when(shift <= SHIFT_BOUND_MAX)
    def _():
        n_groups = n_chunks // ATT_UNROLL
        n_rest = n_chunks - n_groups * ATT_UNROLL

        for rest in range(0, ATT_UNROLL, tq // tk):
            @pl.when(n_rest == rest)
            def _():
                first = n_groups * (ATT_UNROLL * tk)
                chunks = []
                for u in range(rest):
                    chunks += full_chunk(pl.multiple_of(first + u * tk, tk))
                for hd, slabs in enumerate(shifted_chunks(chunks + diag_chunks)):
                    for slab, pv in enumerate(slabs):
                        acc_ref[hd, :, slab * tk:(slab + 1) * tk] = pv

        def group_body(g, carry):
            base = g * (ATT_UNROLL * tk)
            chunks = []
            for u in range(ATT_UNROLL):
                chunks += full_chunk(pl.multiple_of(base + u * tk, tk))
            for hd, slabs in enumerate(shifted_chunks(chunks)):
                for slab, pv in enumerate(slabs):
                    acc_ref[hd, :, slab * tk:(slab + 1) * tk] += pv
            return carry

        lax.fori_loop(0, n_groups, group_body, 0)

    @pl.when(shift > SHIFT_BOUND_MAX)
    def _():
        m_ref[...] = jnp.full(m_ref.shape, NEG_BIG, jnp.float32)
        acc_ref[...] = jnp.zeros(acc_ref.shape, jnp.float32)
        key = lax.broadcasted_iota(jnp.int32, (tk, tq), 0)
        qry = lax.broadcasted_iota(jnp.int32, (tk, tq), 1)

        def body(j, carry):
            start = pl.multiple_of(j * tk, tk)
            visible = key + (start - diag) <= qry
            scores = [jnp.dot(k_ref[0, hd, pl.ds(start, tk), :], q_ref[0, hd],
                              preferred_element_type=jnp.float32) for hd in range(ATT_HG)]
            for hd, s in enumerate(scores):
                s = jnp.where(visible, s, NEG_BIG)
                m_old = m_ref[hd]
                m_new = jnp.maximum(m_old, jnp.max(s, axis=0, keepdims=True))
                p = jnp.exp2(s - m_new).astype(jnp.bfloat16)
                alpha = jnp.exp2(m_old - m_new)
                pv = jnp.dot(v_ref[0, hd, :, pl.ds(start, tk)], p, preferred_element_type=jnp.float32)
                acc_ref[hd] = alpha * acc_ref[hd] + pv
                m_ref[hd] = m_new
            return carry

        lax.fori_loop(0, n_chunks + n_slabs, body, 0)

    for hd in range(ATT_HG):
        acc = acc_ref[hd]
        o_ref[0, hd] = (acc[0:V_HEAD_DIM] / acc[V_HEAD_DIM:V_HEAD_DIM + 1]).astype(o_ref.dtype)


def _attention(shift, q_t, k, v_t):
    b, nh, _, s = q_t.shape
    tq = ATT_TQ
    hg = ATT_HG
    return pl.pallas_call(
        _attn_kernel,
        grid=(b, nh // hg, s // tq),
        in_specs=[pl.BlockSpec(memory_space=pltpu.SMEM),
                  pl.BlockSpec((1, hg, HEAD_PAD, tq), lambda bi, g, i: (bi, g, 0, i)),
                  pl.BlockSpec((1, hg, s, HEAD_PAD), lambda bi, g, i: (bi, g, 0, 0)),
                  pl.BlockSpec((1, hg, V_ROWS, s), lambda bi, g, i: (bi, g, 0, 0))],
        out_specs=pl.BlockSpec((1, hg, V_HEAD_DIM, tq), lambda bi, g, i: (bi, g, 0, i)),
        out_shape=jax.ShapeDtypeStruct((b, nh, V_HEAD_DIM, s), jnp.bfloat16),
        scratch_shapes=[pltpu.VMEM((hg, 1, tq), jnp.float32),
                        pltpu.VMEM((hg, V_ROWS, tq), jnp.float32)],
        compiler_params=pltpu.CompilerParams(
            dimension_semantics=("arbitrary", "arbitrary", "arbitrary"), vmem_limit_bytes=VMEM_LIMIT),
        name="mla_attention",
    )(shift, q_t, k, v_t)


def _post_kernel(x_ref, mod_ref, nw_ref, o_ref, g0_ref, mc_ref, wb0_ref, wout_ref, w13_ref, w2_ref, y_ref):
    rows = x_ref.shape[1] // POST_SPLIT
    gate = mod_ref[0, 5:6, :]
    xs = []
    for r in range(POST_SPLIT):
        sl = slice(r * rows, (r + 1) * rows)
        y_attn = lax.dot_general(o_ref[0, :, sl], wb0_ref[...], (((0,), (0,)), ((), ())),
                                 preferred_element_type=jnp.float32)
        merged = g0_ref[0, sl, :].astype(jnp.float32) * y_attn + mc_ref[0, sl, :].astype(jnp.float32)
        y = jnp.dot(merged.astype(jnp.bfloat16), wout_ref[...], preferred_element_type=jnp.float32)
        xs.append(x_ref[0, sl, :] + gate * y)
    _ffn_rows(xs, 2, mod_ref, nw_ref, w13_ref, w2_ref, y_ref)


def _post(x, mod, norm_w, o_t, g0, mc, wb0, wout, w13, w2):
    b, s, d = x.shape
    tm = POST_TM
    rows = o_t.shape[1]
    return pl.pallas_call(
        _post_kernel,
        grid=(b, s // tm),
        in_specs=[pl.BlockSpec((1, tm, d), lambda i, j: (i, j, 0)),
                  pl.BlockSpec((1, 3 * N_SUB, d), lambda i, j: (i, 0, 0)),
                  _resident(norm_w.shape),
                  pl.BlockSpec((1, rows, tm), lambda i, j: (i, 0, j)),
                  pl.BlockSpec((1, tm, d), lambda i, j: (i, j, 0)),
                  pl.BlockSpec((1, tm, d), lambda i, j: (i, j, 0)),
                  _resident(wb0.shape), _resident(wout.shape), _resident(w13.shape), _resident(w2.shape)],
        out_specs=pl.BlockSpec((1, tm, d), lambda i, j: (i, j, 0)),
        out_shape=jax.ShapeDtypeStruct(x.shape, jnp.float32),
        compiler_params=pltpu.CompilerParams(
            dimension_semantics=("arbitrary", "arbitrary"), vmem_limit_bytes=VMEM_LIMIT),
        name="merge_out_ffn2",
    )(x, mod, norm_w, o_t, g0, mc, wb0, wout, w13, w2)


def _score_bound(wqn, wqr, wkn, wkr):
    def sq(w_nope, w_rope):
        return QK_NOPE_DIM * jnp.max(jnp.square(w_nope)) + QK_ROPE_DIM * jnp.max(jnp.square(w_rope))
    bound = Q_SCALE * jnp.sqrt(sq(wqn, wqr) * sq(wkn, wkr)) * ROUNDING_SLACK
    return bound.reshape(1).astype(jnp.float32)


def _bcast_cols(v):
    return jnp.broadcast_to(v[:, None], (v.shape[0], LANES))


def kernel(x, c, positions, w_ada, b_ada, norm_w, ffn_w13, ffn_w2, w_in, q_a_norm, w_uq, kv_a_norm,
           w_ukv, q_norm_nope, k_norm_nope, q_norm_rope, k_norm_rope, conv_w, w_branch, w_out):
    bf = jnp.bfloat16
    b, s, d = x.shape
    depth = w_ada.shape[0]
    cos_t, sin_t = _rope_tables(positions)
    for l in range(depth):
        mod = _modulation(c, w_ada, b_ada, l).reshape(b, 3 * N_SUB, d)

        wi = w_in[l]
        wa_t = wi[:, :A_ROWS].T.astype(bf)
        wcg = wi[:, A_ROWS:].astype(bf)
        wuq = w_uq[l].reshape(Q_LORA_RANK, N_HEADS, QK_NOPE_DIM + QK_ROPE_DIM)
        wuq = jnp.pad(wuq, ((0, 0), (0, 0), (0, HEAD_PAD - QK_NOPE_DIM - QK_ROPE_DIM)))
        wuq_t = wuq.reshape(Q_LORA_RANK, N_HEADS * HEAD_PAD).T.astype(bf)
        wukv_t = w_ukv[l].T.astype(bf)
        colw = jnp.concatenate([_bcast_cols(q_a_norm[l]), _bcast_cols(kv_a_norm[l]),
                                _bcast_cols(q_norm_nope[l] * Q_SCALE), _bcast_cols(k_norm_nope[l]),
                                _bcast_cols(q_norm_rope[l] * Q_SCALE), _bcast_cols(k_norm_rope[l])], axis=0)
        shift = _score_bound(q_norm_nope[l], q_norm_rope[l], k_norm_nope[l], k_norm_rope[l])

        x = _ffn(x, mod, norm_w[l], ffn_w13[l, 0].astype(bf), ffn_w2[l, 0].astype(bf), 0)
        q_t, k, v_t, g0, mc = _mixer_inputs(shift, x, mod, norm_w[l], cos_t, sin_t, wa_t, wcg, wuq_t, wukv_t,
                                            colw, conv_w[l], w_branch[l, 1].astype(bf))
        o_t = _attention(shift, q_t, k, v_t)
        x = _post(x, mod, norm_w[l], o_t.reshape(b, N_HEADS * V_HEAD_DIM, s), g0, mc,
                  w_branch[l, 0].astype(bf), w_out[l].astype(bf),
                  ffn_w13[l, 1].astype(bf), ffn_w2[l, 1].astype(bf))
    return x
```

```python
import functools
import math

import jax
import jax.numpy as jnp
from jax import lax
from jax.experimental import pallas as pl
from jax.experimental.pallas import tpu as pltpu

D_MODEL = 1024
N_HEADS = 8
QK_NOPE_DIM = 64
QK_ROPE_DIM = 32
V_HEAD_DIM = 64
Q_LORA_RANK = 256
KV_LORA_RANK = 128
CONV_WIDTH = 512
CONV_K = 3
D_FF = 2816
N_SUB = 3
ROPE_THETA = 10000.0
EPS = 1e-6

LANES = 128
HEAD_PAD = 128
ROPE_HALF = QK_ROPE_DIM // 2
V_ROWS = V_HEAD_DIM + 16
A_ROWS = Q_LORA_RANK + KV_LORA_RANK + QK_ROPE_DIM
CG_COLS = 3 * CONV_WIDTH + 2 * D_MODEL
VMEM_LIMIT = 56 * 1024 * 1024

FFN_TM = 1024
FFN_SPLIT = 4
MIX_TM = 1024
POST_TM = 1024
POST_SPLIT = 4
ATT_TQ = 2048
ATT_TK = 256
ATT_HG = 4
ATT_UNROLL = 8
ATT_LAG = 8

Q_SCALE = (QK_NOPE_DIM + QK_ROPE_DIM) ** -0.5 * math.log2(math.e)
NEG_BIG = -1e30
SHIFT_BOUND_MAX = 50.0
ROUNDING_SLACK = 1.01


def _resident(shape):
    n = len(shape)
    return pl.BlockSpec(shape, lambda *_: (0,) * n, pipeline_mode=pl.Buffered(1))


def _modulated_norm(x, nw, shift, scale):
    ms = jnp.mean(x * x, axis=-1, keepdims=True)
    return x * lax.rsqrt(ms + EPS) * (nw * (1.0 + scale)) + shift


def _rope_kernel(pos_ref, invf_ref, cos_ref, sin_ref):
    ang = pos_ref[0].astype(jnp.float32) * invf_ref[...]
    cos_ref[0] = jnp.cos(ang)
    sin_ref[0] = jnp.sin(ang)


def _rope_tables(positions):
    b, s = positions.shape
    inv_freq = 1.0 / (ROPE_THETA ** (jnp.arange(0, QK_ROPE_DIM, 2, dtype=jnp.float32) / QK_ROPE_DIM))
    invf = jnp.broadcast_to(inv_freq[:, None], (ROPE_HALF, s))
    out = jax.ShapeDtypeStruct((b, ROPE_HALF, s), jnp.float32)
    return pl.pallas_call(
        _rope_kernel,
        grid=(b,),
        in_specs=[pl.BlockSpec((1, 1, s), lambda i: (i, 0, 0)),
                  pl.BlockSpec((ROPE_HALF, s), lambda i: (0, 0))],
        out_specs=[pl.BlockSpec((1, ROPE_HALF, s), lambda i: (i, 0, 0))] * 2,
        out_shape=[out, out],
        name="rope_tables",
    )(positions.reshape(b, 1, s), invf)


def _mod_kernel(c_ref, w_ref, b_ref, o_ref):
    c = c_ref[...]
    c_act = c * jax.nn.sigmoid(c)
    o_ref[...] = jnp.dot(c_act.astype(jnp.bfloat16), w_ref[0].astype(jnp.bfloat16),
                         preferred_element_type=jnp.float32) + b_ref[0]


def _modulation(c, w_ada, b_ada, layer):
    b, d = c.shape
    depth, _, n = w_ada.shape
    tn = 1024
    return pl.pallas_call(
        _mod_kernel,
        grid=(n // tn,),
        in_specs=[pl.BlockSpec((b, d), lambda j: (0, 0)),
                  pl.BlockSpec((1, d, tn), lambda j: (layer, 0, j)),
                  pl.BlockSpec((1, 1, tn), lambda j: (layer, 0, j))],
        out_specs=pl.BlockSpec((b, tn), lambda j: (0, j)),
        out_shape=jax.ShapeDtypeStruct((b, n), jnp.float32),
        name="adaln_mod",
    )(c, w_ada, b_ada.reshape(depth, 1, n))


def _ffn_rows(xs, sub, mod_ref, nw_ref, w13_ref, w2_ref, o_ref):
    shift = mod_ref[0, 3 * sub + 0:3 * sub + 1, :]
    scale = mod_ref[0, 3 * sub + 1:3 * sub + 2, :]
    gate = 0.5 * mod_ref[0, 3 * sub + 2:3 * sub + 3, :]
    nw = nw_ref[sub:sub + 1, :]
    rows = xs[0].shape[0]
    def up(x):
        h = _modulated_norm(x, nw, shift, scale).astype(jnp.bfloat16)
        return jnp.dot(h, w13_ref[...], preferred_element_type=jnp.float32)

    gu_next = up(xs[0])
    for r, x in enumerate(xs):
        gu = gu_next
        if r + 1 < len(xs):
            gu_next = up(xs[r + 1])
        g = gu[:, :D_FF]
        u = gu[:, D_FF:]
        a = (g * jax.nn.sigmoid(g) * u).astype(jnp.bfloat16)
        y = jnp.dot(a, w2_ref[...], preferred_element_type=jnp.float32)
        o_ref[0, r * rows:(r + 1) * rows, :] = x + gate * y


def _ffn_kernel(sub, x_ref, mod_ref, nw_ref, w13_ref, w2_ref, o_ref):
    rows = x_ref.shape[1] // FFN_SPLIT
    xs = [x_ref[0, r * rows:(r + 1) * rows, :] for r in range(FFN_SPLIT)]
    _ffn_rows(xs, sub, mod_ref, nw_ref, w13_ref, w2_ref, o_ref)


def _ffn(x, mod, norm_w, w13, w2, sub):
    b, s, d = x.shape
    tm = FFN_TM
    return pl.pallas_call(
        functools.partial(_ffn_kernel, sub),
        grid=(b, s // tm),
        in_specs=[pl.BlockSpec((1, tm, d), lambda i, j: (i, j, 0)),
                  pl.BlockSpec((1, 3 * N_SUB, d), lambda i, j: (i, 0, 0)),
                  _resident(norm_w.shape),
                  _resident(w13.shape),
                  _resident(w2.shape)],
        out_specs=pl.BlockSpec((1, tm, d), lambda i, j: (i, j, 0)),
        out_shape=jax.ShapeDtypeStruct(x.shape, jnp.float32),
        compiler_params=pltpu.CompilerParams(
            dimension_semantics=("arbitrary", "arbitrary"), vmem_limit_bytes=VMEM_LIMIT),
        name=f"ffn{sub}",
    )(x, mod, norm_w, w13, w2)


def _rms_rows(x, w):
    ms = jnp.mean(x * x, axis=0, keepdims=True)
    return x * lax.rsqrt(ms + EPS) * w


def _rope_rows(y, cos, sin):
    y1 = y[:ROPE_HALF]
    y2 = y[ROPE_HALF:]
    return y1 * cos - y2 * sin, y2 * cos + y1 * sin


def _mix_kernel(shift_ref, x_ref, mod_ref, nw_ref, cos_ref, sin_ref, wa_ref, wcg_ref, wuq_ref, wukv_ref,
                colw_ref, convw_ref, wb1_ref,
                q_ref, k_ref, v_ref, g0_ref, mc_ref, tail_ref):
    tm = x_ref.shape[1]
    rep = tm // LANES
    j = pl.program_id(1)

    @pl.when(j == 0)
    def _():
        tail_ref[...] = jnp.zeros_like(tail_ref)

    half = tm // 2
    hs = [_modulated_norm(x_ref[0, r * half:(r + 1) * half, :], nw_ref[1:2, :], mod_ref[0, 3:4, :],
                          mod_ref[0, 4:5, :]).astype(jnp.bfloat16) for r in range(2)]

    def rows_dot(w):
        return jnp.concatenate([jnp.dot(h, w, preferred_element_type=jnp.float32) for h in hs], axis=0)

    zt = jnp.concatenate([lax.dot_general(wa_ref[...], h, (((1,), (1,)), ((), ())),
                                          preferred_element_type=jnp.float32) for h in hs],
                         axis=1)
    cv = rows_dot(wcg_ref[:, 0:3 * CONV_WIDTH])

    def colw(lo, n):
        return jnp.tile(colw_ref[lo:lo + n, :], (1, rep))

    o_qa, o_kva = 0, Q_LORA_RANK
    o_qn = o_kva + KV_LORA_RANK
    o_kn = o_qn + QK_NOPE_DIM
    o_qr = o_kn + QK_NOPE_DIM
    o_kr = o_qr + QK_ROPE_DIM

    cq = _rms_rows(zt[0:Q_LORA_RANK], colw(o_qa, Q_LORA_RANK)).astype(jnp.bfloat16)
    ckv = _rms_rows(zt[Q_LORA_RANK:Q_LORA_RANK + KV_LORA_RANK], colw(o_kva, KV_LORA_RANK)).astype(jnp.bfloat16)
    kr_raw = zt[Q_LORA_RANK + KV_LORA_RANK:A_ROWS]
    cos = cos_ref[0]
    sin = sin_ref[0]
    kr1, kr2 = _rope_rows(_rms_rows(kr_raw, colw(o_kr, QK_ROPE_DIM)), cos, sin)

    gl0 = rows_dot(wcg_ref[:, 3 * CONV_WIDTH:3 * CONV_WIDTH + D_MODEL])
    qt = jnp.dot(wuq_ref[...], cq, preferred_element_type=jnp.float32)
    kvt = jnp.dot(wukv_ref[...], ckv, preferred_element_type=jnp.float32)
    gl1 = rows_dot(wcg_ref[:, 3 * CONV_WIDTH + D_MODEL:CG_COLS])

    xv = cv[:, 0:CONV_WIDTH]
    gate_b = cv[:, CONV_WIDTH:2 * CONV_WIDTH]
    gate_c = cv[:, 2 * CONV_WIDTH:3 * CONV_WIDTH]
    u = gate_c * xv

    p1 = tail_ref[7:8, :]
    p2 = tail_ref[6:7, :]
    row = lax.broadcasted_iota(jnp.int32, u.shape, 0)
    r1 = jnp.where(row == 0, p1, pltpu.roll(u, 1, 0))
    r2 = jnp.where(row == 0, p2, jnp.where(row == 1, p1, pltpu.roll(u, 2, 0)))
    tail_ref[...] = u[tm - 8:tm, :]
    conv = convw_ref[0:1, :] * r2 + convw_ref[1:2, :] * r1 + convw_ref[2:3, :] * u
    o_conv = (gate_b * conv).astype(jnp.bfloat16)
    y_conv = jnp.dot(o_conv, wb1_ref[...], preferred_element_type=jnp.float32)
    g0_ref[0] = jax.nn.sigmoid(gl0).astype(jnp.bfloat16)
    mc_ref[0] = (jax.nn.sigmoid(gl1) * y_conv).astype(jnp.bfloat16)

    wqn = colw(o_qn, QK_NOPE_DIM)
    wkn = colw(o_kn, QK_NOPE_DIM)
    wqr = colw(o_qr, QK_ROPE_DIM)
    shift = shift_ref[0]
    folded = jnp.where(shift <= SHIFT_BOUND_MAX, shift, 0.0)
    pad_row = lax.broadcasted_iota(jnp.int32, (HEAD_PAD - QK_NOPE_DIM - QK_ROPE_DIM, tm), 0)
    k_pad = jnp.where(pad_row == 0, 1.0, 0.0)
    q_pad = jnp.where(pad_row == 0, -folded, 0.0).astype(jnp.bfloat16)
    ones = jnp.ones((V_ROWS - V_HEAD_DIM, tm), jnp.bfloat16)
    for hd in range(N_HEADS):
        base = hd * HEAD_PAD
        qn = _rms_rows(qt[base:base + QK_NOPE_DIM], wqn)
        qr1, qr2 = _rope_rows(
            _rms_rows(qt[base + QK_NOPE_DIM:base + QK_NOPE_DIM + QK_ROPE_DIM], wqr), cos, sin)
        q_ref[0, hd, 0:QK_NOPE_DIM, :] = qn.astype(jnp.bfloat16)
        q_ref[0, hd, QK_NOPE_DIM:QK_NOPE_DIM + ROPE_HALF, :] = qr1.astype(jnp.bfloat16)
        q_ref[0, hd, QK_NOPE_DIM + ROPE_HALF:QK_NOPE_DIM + QK_ROPE_DIM, :] = qr2.astype(jnp.bfloat16)
        q_ref[0, hd, QK_NOPE_DIM + QK_ROPE_DIM:HEAD_PAD, :] = q_pad

        kn = _rms_rows(kvt[base:base + QK_NOPE_DIM], wkn)
        kt = jnp.concatenate([kn, kr1, kr2, k_pad], axis=0)
        k_ref[0, hd] = kt.T.astype(jnp.bfloat16)

        v_ref[0, hd, 0:V_HEAD_DIM, :] = kvt[base + QK_NOPE_DIM:base + HEAD_PAD].astype(jnp.bfloat16)
        v_ref[0, hd, V_HEAD_DIM:V_ROWS, :] = ones


def _mixer_inputs(shift, x, mod, norm_w, cos_t, sin_t, wa_t, wcg, wuq_t, wukv_t, colw, conv_w, wb1):
    b, s, d = x.shape
    tm = MIX_TM
    bf = jnp.bfloat16
    out_shape = [
        jax.ShapeDtypeStruct((b, N_HEADS, HEAD_PAD, s), bf),
        jax.ShapeDtypeStruct((b, N_HEADS, s, HEAD_PAD), bf),
        jax.ShapeDtypeStruct((b, N_HEADS, V_ROWS, s), bf),
        jax.ShapeDtypeStruct((b, s, d), bf),
        jax.ShapeDtypeStruct((b, s, d), bf),
    ]
    out_specs = [
        pl.BlockSpec((1, N_HEADS, HEAD_PAD, tm), lambda i, j: (i, 0, 0, j)),
        pl.BlockSpec((1, N_HEADS, tm, HEAD_PAD), lambda i, j: (i, 0, j, 0)),
        pl.BlockSpec((1, N_HEADS, V_ROWS, tm), lambda i, j: (i, 0, 0, j)),
        pl.BlockSpec((1, tm, d), lambda i, j: (i, j, 0)),
        pl.BlockSpec((1, tm, d), lambda i, j: (i, j, 0)),
    ]
    in_specs = [
        pl.BlockSpec(memory_space=pltpu.SMEM),
        pl.BlockSpec((1, tm, d), lambda i, j: (i, j, 0)),
        pl.BlockSpec((1, 3 * N_SUB, d), lambda i, j: (i, 0, 0)),
        _resident(norm_w.shape),
        pl.BlockSpec((1, ROPE_HALF, tm), lambda i, j: (i, 0, j)),
        pl.BlockSpec((1, ROPE_HALF, tm), lambda i, j: (i, 0, j)),
        _resident(wa_t.shape), _resident(wcg.shape), _resident(wuq_t.shape), _resident(wukv_t.shape),
        _resident(colw.shape), _resident(conv_w.shape), _resident(wb1.shape),
    ]
    return pl.pallas_call(
        _mix_kernel,
        grid=(b, s // tm),
        in_specs=in_specs,
        out_specs=out_specs,
        out_shape=out_shape,
        scratch_shapes=[pltpu.VMEM((8, CONV_WIDTH), jnp.float32)],
        compiler_params=pltpu.CompilerParams(
            dimension_semantics=("arbitrary", "arbitrary"), vmem_limit_bytes=VMEM_LIMIT),
        name="mixer_inputs",
    )(shift, x, mod, norm_w, cos_t, sin_t, wa_t, wcg, wuq_t, wukv_t, colw, conv_w, wb1)


def _attn_kernel(shift_ref, q_ref, k_ref, v_ref, o_ref, m_ref, acc_ref):
    tq = q_ref.shape[3]
    tk = ATT_TK
    i = pl.program_id(2)
    diag = pl.multiple_of(i * tq, tq)
    n_chunks = i * (tq // tk)
    shift = shift_ref[0]

    def causal_mask(rows, cols):
        key = lax.broadcasted_iota(jnp.int32, (rows, cols), 0)
        qry = lax.broadcasted_iota(jnp.int32, (rows, cols), 1)
        return key <= qry

    n_slabs = tq // tk

    def shifted_chunks(chunks):
        items = [(c, hd) for c in range(len(chunks)) for hd in range(ATT_HG)]
        pending = {}
        out = [[None] * n_slabs for _ in range(ATT_HG)]
        for n in range(len(items) + ATT_LAG):
            if n < len(items):
                c, hd = items[n]
                start, mask, slab = chunks[c]
                s = jnp.dot(k_ref[0, hd, pl.ds(start, tk), :], q_ref[0, hd, :, slab * tk:(slab + 1) * tk],
                            preferred_element_type=jnp.float32)
                pending[n] = s if mask is None else jnp.where(mask, s, NEG_BIG)
            if n >= ATT_LAG:
                c, hd = items[n - ATT_LAG]
                start, _, slab = chunks[c]
                p = jnp.exp2(pending.pop(n - ATT_LAG)).astype(jnp.bfloat16)
                pv = jnp.dot(v_ref[0, hd, :, pl.ds(start, tk)], p, preferred_element_type=jnp.float32)
                out[hd][slab] = pv if out[hd][slab] is None else out[hd][slab] + pv
        return out

    def full_chunk(start):
        return [(start, None, slab) for slab in range(n_slabs)]

    diag_chunks = []
    for d in range(n_slabs):
        start = pl.multiple_of(diag + d * tk, tk)
        diag_chunks.append((start, causal_mask(tk, tk), d))
        diag_chunks += [(start, None, slab) for slab in range(d + 1, n_slabs)]

    @pl.when(shift <= SHIFT_BOUND_MAX)
    def _():
        n_groups = n_chunks // ATT_UNROLL
        n_rest = n_chunks - n_groups * ATT_UNROLL

        for rest in range(0, ATT_UNROLL, tq // tk):
            @pl.when(n_rest == rest)
            def _():
                first = n_groups * (ATT_UNROLL * tk)
                chunks = []
                for u in range(rest):
                    chunks += full_chunk(pl.multiple_of(first + u * tk, tk))
                for hd, slabs in enumerate(shifted_chunks(chunks + diag_chunks)):
                    for slab, pv in enumerate(slabs):
                        acc_ref[hd, :, slab * tk:(slab + 1) * tk] = pv

        def group_body(g, carry):
            base = g * (ATT_UNROLL * tk)
            chunks = []
            for u in range(ATT_UNROLL):
                chunks += full_chunk(pl.multiple_of(base + u * tk, tk))
            for hd, slabs in enumerate(shifted_chunks(chunks)):
                for slab, pv in enumerate(slabs):
                    acc_ref[hd, :, slab * tk:(slab + 1) * tk] += pv
            return carry

        lax.fori_loop(0, n_groups, group_body, 0)

    @pl.when(shift > SHIFT_BOUND_MAX)
    def _():
        m_ref[...] = jnp.full(m_ref.shape, NEG_BIG, jnp.float32)
        acc_ref[...] = jnp.zeros(acc_ref.shape, jnp.float32)
        key = lax.broadcasted_iota(jnp.int32, (tk, tq), 0)
        qry = lax.broadcasted_iota(jnp.int32, (tk, tq), 1)

        def body(j, carry):
            start = pl.multiple_of(j * tk, tk)
            visible = key + (start - diag) <= qry
            scores = [jnp.dot(k_ref[0, hd, pl.ds(start, tk), :], q_ref[0, hd],
                              preferred_element_type=jnp.float32) for hd in range(ATT_HG)]
            for hd, s in enumerate(scores):
                s = jnp.where(visible, s, NEG_BIG)
                m_old = m_ref[hd]
                m_new = jnp.maximum(m_old, jnp.max(s, axis=0, keepdims=True))
                p = jnp.exp2(s - m_new).astype(jnp.bfloat16)
                alpha = jnp.exp2(m_old - m_new)
                pv = jnp.dot(v_ref[0, hd, :, pl.ds(start, tk)], p, preferred_element_type=jnp.float32)
                acc_ref[hd] = alpha * acc_ref[hd] + pv
                m_ref[hd] = m_new
            return carry

        lax.fori_loop(0, n_chunks + n_slabs, body, 0)

    for hd in range(ATT_HG):
        acc = acc_ref[hd]
        o_ref[0, hd] = (acc[0:V_HEAD_DIM] / acc[V_HEAD_DIM:V_HEAD_DIM + 1]).astype(o_ref.dtype)


def _attention(shift, q_t, k, v_t):
    b, nh, _, s = q_t.shape
    tq = ATT_TQ
    hg = ATT_HG
    return pl.pallas_call(
        _attn_kernel,
        grid=(b, nh // hg, s // tq),
        in_specs=[pl.BlockSpec(memory_space=pltpu.SMEM),
                  pl.BlockSpec((1, hg, HEAD_PAD, tq), lambda bi, g, i: (bi, g, 0, i)),
                  pl.BlockSpec((1, hg, s, HEAD_PAD), lambda bi, g, i: (bi, g, 0, 0)),
                  pl.BlockSpec((1, hg, V_ROWS, s), lambda bi, g, i: (bi, g, 0, 0))],
        out_specs=pl.BlockSpec((1, hg, V_HEAD_DIM, tq), lambda bi, g, i: (bi, g, 0, i)),
        out_shape=jax.ShapeDtypeStruct((b, nh, V_HEAD_DIM, s), jnp.bfloat16),
        scratch_shapes=[pltpu.VMEM((hg, 1, tq), jnp.float32),
                        pltpu.VMEM((hg, V_ROWS, tq), jnp.float32)],
        compiler_params=pltpu.CompilerParams(
            dimension_semantics=("arbitrary", "arbitrary", "arbitrary"), vmem_limit_bytes=VMEM_LIMIT),
        name="mla_attention",
    )(shift, q_t, k, v_t)


def _post_kernel(x_ref, mod_ref, nw_ref, o_ref, g0_ref, mc_ref, wb0_ref, wout_ref, w13_ref, w2_ref, y_ref):
    rows = x_ref.shape[1] // POST_SPLIT
    gate = mod_ref[0, 5:6, :]
    xs = []
    for r in range(POST_SPLIT):
        sl = slice(r * rows, (r + 1) * rows)
        y_attn = lax.dot_general(o_ref[0, :, sl], wb0_ref[...], (((0,), (0,)), ((), ())),
                                 preferred_element_type=jnp.float32)
        merged = g0_ref[0, sl, :].astype(jnp.float32) * y_attn + mc_ref[0, sl, :].astype(jnp.float32)
        y = jnp.dot(merged.astype(jnp.bfloat16), wout_ref[...], preferred_element_type=jnp.float32)
        xs.append(x_ref[0, sl, :] + gate * y)
    _ffn_rows(xs, 2, mod_ref, nw_ref, w13_ref, w2_ref, y_ref)


def _post(x, mod, norm_w, o_t, g0, mc, wb0, wout, w13, w2):
    b, s, d = x.shape
    tm = POST_TM
    rows = o_t.shape[1]
    return pl.pallas_call(
        _post_kernel,
        grid=(b, s // tm),
        in_specs=[pl.BlockSpec((1, tm, d), lambda i, j: (i, j, 0)),
                  pl.BlockSpec((1, 3 * N_SUB, d), lambda i, j: (i, 0, 0)),
                  _resident(norm_w.shape),
                  pl.BlockSpec((1, rows, tm), lambda i, j: (i, 0, j)),
                  pl.BlockSpec((1, tm, d), lambda i, j: (i, j, 0)),
                  pl.BlockSpec((1, tm, d), lambda i, j: (i, j, 0)),
                  _resident(wb0.shape), _resident(wout.shape), _resident(w13.shape), _resident(w2.shape)],
        out_specs=pl.BlockSpec((1, tm, d), lambda i, j: (i, j, 0)),
        out_shape=jax.ShapeDtypeStruct(x.shape, jnp.float32),
        compiler_params=pltpu.CompilerParams(
            dimension_semantics=("arbitrary", "arbitrary"), vmem_limit_bytes=VMEM_LIMIT),
        name="merge_out_ffn2",
    )(x, mod, norm_w, o_t, g0, mc, wb0, wout, w13, w2)


def _score_bound(wqn, wqr, wkn, wkr):
    def sq(w_nope, w_rope):
        return QK_NOPE_DIM * jnp.max(jnp.square(w_nope)) + QK_ROPE_DIM * jnp.max(jnp.square(w_rope))
    bound = Q_SCALE * jnp.sqrt(sq(wqn, wqr) * sq(wkn, wkr)) * ROUNDING_SLACK
    return bound.reshape(1).astype(jnp.float32)


def _bcast_cols(v):
    return jnp.broadcast_to(v[:, None], (v.shape[0], LANES))


def kernel(x, c, positions, w_ada, b_ada, norm_w, ffn_w13, ffn_w2, w_in, q_a_norm, w_uq, kv_a_norm,
           w_ukv, q_norm_nope, k_norm_nope, q_norm_rope, k_norm_rope, conv_w, w_branch, w_out):
    bf = jnp.bfloat16
    b, s, d = x.shape
    depth = w_ada.shape[0]
    cos_t, sin_t = _rope_tables(positions)
    for l in range(depth):
        mod = _modulation(c, w_ada, b_ada, l).reshape(b, 3 * N_SUB, d)

        wi = w_in[l]
        wa_t = wi[:, :A_ROWS].T.astype(bf)
        wcg = wi[:, A_ROWS:].astype(bf)
        wuq = w_uq[l].reshape(Q_LORA_RANK, N_HEADS, QK_NOPE_DIM + QK_ROPE_DIM)
        wuq = jnp.pad(wuq, ((0, 0), (0, 0), (0, HEAD_PAD - QK_NOPE_DIM - QK_ROPE_DIM)))
        wuq_t = wuq.reshape(Q_LORA_RANK, N_HEADS * HEAD_PAD).T.astype(bf)
        wukv_t = w_ukv[l].T.astype(bf)
        colw = jnp.concatenate([_bcast_cols(q_a_norm[l]), _bcast_cols(kv_a_norm[l]),
                                _bcast_cols(q_norm_nope[l] * Q_SCALE), _bcast_cols(k_norm_nope[l]),
                                _bcast_cols(q_norm_rope[l] * Q_SCALE), _bcast_cols(k_norm_rope[l])], axis=0)
        shift = _score_bound(q_norm_nope[l], q_norm_rope[l], k_norm_nope[l], k_norm_rope[l])

        x = _ffn(x, mod, norm_w[l], ffn_w13[l, 0].astype(bf), ffn_w2[l, 0].astype(bf), 0)
        q_t, k, v_t, g0, mc = _mixer_inputs(shift, x, mod, norm_w[l], cos_t, sin_t, wa_t, wcg, wuq_t, wukv_t,
                                            colw, conv_w[l], w_branch[l, 1].astype(bf))
        o_t = _attention(shift, q_t, k, v_t)
        x = _post(x, mod, norm_w[l], o_t.reshape(b, N_HEADS * V_HEAD_DIM, s), g0, mc,
                  w_branch[l, 0].astype(bf), w_out[l].astype(bf),
                  ffn_w13[l, 1].astype(bf), ffn_w2[l, 1].astype(bf))
    return x
```
